```python
import jax
import jax.numpy as jnp
from jax import lax
import numpy as np

D_MODEL = 1024
BATCH = 4
SEQ = 4096
DEPTH = 4

D_PLE = 256
N_SG = 8
SG_DIM = 64
D_SG = N_SG * SG_DIM
CHUNK = 128
N_HEADS = 8
QK_NOPE = 64
QK_ROPE = 32
V_DIM = 64
Q_LORA = 256
KV_LORA = 128
D_ATT = N_HEADS * V_DIM
D_MIX = D_SG + D_ATT
D_IN = 2 * D_SG + Q_LORA + KV_LORA + QK_ROPE
ROPE_THETA = 10000.0
Q_BLOCK = 128
D_FF = 2816
N_EXPERTS = 8
TOP_K = 2
D_FF_EXPERT = 3584
N_DENSE = (DEPTH + 1) // 2
N_MOE = DEPTH // 2
DN_ALPHA = (2.0 * DEPTH) ** 0.25
DN_BETA = (8.0 * DEPTH) ** -0.25
EPS = 1e-6

kernel_name = "hybrid_gmlp_mla_moe_deepnorm"


def layer_norm(x, g, b):
    xf = x.astype(jnp.float32)
    mu = jnp.mean(xf, -1, keepdims=True)
    var = jnp.mean(jnp.square(xf - mu), -1, keepdims=True)
    return ((xf - mu) * lax.rsqrt(var + EPS) * g + b).astype(x.dtype)


def rms_norm(x, g):
    xf = x.astype(jnp.float32)
    return (xf * lax.rsqrt(jnp.mean(xf * xf, -1, keepdims=True) + EPS) * g).astype(x.dtype)


def rope_angles(positions):
    inv_freq = 1.0 / (ROPE_THETA ** (jnp.arange(0, QK_ROPE, 2, dtype=jnp.float32) / QK_ROPE))
    ang = positions.astype(jnp.float32)[..., None] * inv_freq
    return jnp.cos(ang), jnp.sin(ang)


def apply_rope(x, cos, sin):
    xf = x.astype(jnp.float32)
    x1, x2 = jnp.split(xf, 2, axis=-1)
    return jnp.concatenate([x1 * cos - x2 * sin, x1 * sin + x2 * cos], -1).astype(x.dtype)


def spatial_gating(uv, v_g, v_b, w_s, b_s):
    B, S, _ = uv.shape
    u, v = jnp.split(jax.nn.gelu(uv, approximate=False), 2, axis=-1)
    v = layer_norm(v, v_g, v_b)
    v = v.reshape(B, S // CHUNK, CHUNK, N_SG, SG_DIM)
    causal = jnp.tril(jnp.ones((CHUNK, CHUNK), dtype=bool))
    w = jnp.where(causal[None], w_s, 0)
    mixed = jnp.einsum('gts,bcsgd->bctgd', w, v) + b_s.T[None, None, :, :, None]
    return u * mixed.reshape(B, S, D_SG)


def latent_attention(c_q, c_kv, k_r, q_g, kv_g, w_uq, w_ukv, cos, sin):
    B, S, _ = c_q.shape
    q = (rms_norm(c_q, q_g) @ w_uq).reshape(B, S, N_HEADS, QK_NOPE + QK_ROPE)
    q_nope = q[..., :QK_NOPE]
    q_rope = apply_rope(q[..., QK_NOPE:], cos[:, :, None, :], sin[:, :, None, :])
    kv = (rms_norm(c_kv, kv_g) @ w_ukv).reshape(B, S, N_HEADS, QK_NOPE + V_DIM)
    k_nope = kv[..., :QK_NOPE]
    v = kv[..., QK_NOPE:]
    k_rope = apply_rope(k_r, cos, sin)
    scale = (QK_NOPE + QK_ROPE) ** -0.5
    n_blk = S // Q_BLOCK
    qn_b = q_nope.reshape(B, n_blk, Q_BLOCK, N_HEADS, QK_NOPE).transpose(1, 0, 2, 3, 4)
    qr_b = q_rope.reshape(B, n_blk, Q_BLOCK, N_HEADS, QK_ROPE).transpose(1, 0, 2, 3, 4)
    k_pos = jnp.arange(S)

    def block(args):
        qn, qr, bi = args
        s = (jnp.einsum('bqhd,bkhd->bhqk', qn, k_nope)
             + jnp.einsum('bqhd,bkd->bhqk', qr, k_rope)).astype(jnp.float32) * scale
        q_pos = bi * Q_BLOCK + jnp.arange(Q_BLOCK)
        mask = k_pos[None, :] <= q_pos[:, None]
        s = jnp.where(mask[None, None], s, -jnp.inf)
        pr = jax.nn.softmax(s, axis=-1).astype(v.dtype)
        return jnp.einsum('bhqk,bkhd->bqhd', pr, v)

    o = lax.map(block, (qn_b, qr_b, jnp.arange(n_blk)))
    return o.transpose(1, 0, 2, 3, 4).reshape(B, S, D_ATT)


def swiglu(x, w1, w3, w2):
    return (jax.nn.silu(x @ w1) * (x @ w3)) @ w2


def moe(x, w_r, w1, w3, w2):
    B, S, D = x.shape
    xt = x.reshape(-1, D)
    logits = (xt @ w_r).astype(jnp.float32)
    top_l, top_i = lax.top_k(logits, TOP_K)
    top_w = jax.nn.softmax(top_l, axis=-1)
    combine = jnp.sum(jax.nn.one_hot(top_i, N_EXPERTS, dtype=jnp.float32) * top_w[..., None], axis=1)
    combine = combine.astype(x.dtype)
    out = jnp.zeros_like(xt)
    for e in range(N_EXPERTS):
        out = out + combine[:, e:e + 1] * swiglu(xt, w1[e], w3[e], w2[e])
    return out.reshape(B, S, D)


def setup_inputs(seed: int = 0) -> dict:
    key = jax.random.key(seed)
    ks = iter(jax.random.split(key, 32))
    L = DEPTH

    def nrm(shape, scale):
        return jax.random.normal(next(ks), shape, jnp.float32) * scale

    def gain(shape):
        return 1.0 + nrm(shape, 0.02)

    x = nrm((BATCH, SEQ, D_MODEL), 1.0)
    p = nrm((DEPTH, BATCH, SEQ, D_PLE), 1.0)
    offs = jax.random.randint(next(ks), (BATCH, 1), 0, 1024, dtype=jnp.int32)
    positions = offs + jnp.arange(SEQ, dtype=jnp.int32)[None, :]
    return {
        'x': x,
        'p': p,
        'positions': positions,
        'w_in': nrm((L, D_MODEL, D_IN), D_MODEL ** -0.5),
        'sg_v_g': gain((L, D_SG)),
        'sg_v_b': nrm((L, D_SG), 0.02),
        'sg_w_s': nrm((L, N_SG, CHUNK, CHUNK), 0.5 * CHUNK ** -0.5),
        'sg_b_s': 1.0 + nrm((L, N_SG, CHUNK), 0.1),
        'q_norm_g': gain((L, Q_LORA)),
        'kv_norm_g': gain((L, KV_LORA)),
        'w_uq': nrm((L, Q_LORA, N_HEADS * (QK_NOPE + QK_ROPE)), Q_LORA ** -0.5),
        'w_ukv': nrm((L, KV_LORA, N_HEADS * (QK_NOPE + V_DIM)), KV_LORA ** -0.5),
        'out_g': gain((L, D_MIX)),
        'w_o': nrm((L, D_MIX, D_MODEL), DN_BETA * D_MIX ** -0.5),
        'ln1_g': gain((L, D_MODEL)),
        'ln1_b': nrm((L, D_MODEL), 0.02),
        'ffn_w1': nrm((N_DENSE, D_MODEL, D_FF), D_MODEL ** -0.5),
        'ffn_w3': nrm((N_DENSE, D_MODEL, D_FF), D_MODEL ** -0.5),
        'ffn_w2': nrm((N_DENSE, D_FF, D_MODEL), DN_BETA * D_FF ** -0.5),
        'moe_w_r': nrm((N_MOE, D_MODEL, N_EXPERTS), D_MODEL ** -0.5),
        'moe_w1': nrm((N_MOE, N_EXPERTS, D_MODEL, D_FF_EXPERT), D_MODEL ** -0.5),
        'moe_w3': nrm((N_MOE, N_EXPERTS, D_MODEL, D_FF_EXPERT), D_MODEL ** -0.5),
        'moe_w2': nrm((N_MOE, N_EXPERTS, D_FF_EXPERT, D_MODEL), DN_BETA * D_FF_EXPERT ** -0.5),
        'ln2_g': gain((L, D_MODEL)),
        'ln2_b': nrm((L, D_MODEL), 0.02),
        'ple_w_g': nrm((L, D_MODEL, D_MODEL), D_MODEL ** -0.5),
        'ple_b_g': nrm((L, D_MODEL), 0.01),
        'ple_w_p': nrm((L, D_PLE, D_MODEL), D_PLE ** -0.5),
    }


def reference(x, p, positions, w_in, sg_v_g, sg_v_b, sg_w_s, sg_b_s, q_norm_g, kv_norm_g,
              w_uq, w_ukv, out_g, w_o, ln1_g, ln1_b, ffn_w1, ffn_w3, ffn_w2,
              moe_w_r, moe_w1, moe_w3, moe_w2, ln2_g, ln2_b, ple_w_g, ple_b_g, ple_w_p):
    cos, sin = rope_angles(positions)
    s1 = 2 * D_SG
    s2 = s1 + Q_LORA
    s3 = s2 + KV_LORA
    for i in range(DEPTH):
        h = x @ w_in[i]
        a = spatial_gating(h[..., :s1], sg_v_g[i], sg_v_b[i], sg_w_s[i], sg_b_s[i])
        m = latent_attention(h[..., s1:s2], h[..., s2:s3], h[..., s3:], q_norm_g[i], kv_norm_g[i],
                             w_uq[i], w_ukv[i], cos, sin)
        y = jnp.concatenate([rms_norm(a, out_g[i, :D_SG]), rms_norm(m, out_g[i, D_SG:])], -1) @ w_o[i]
        x = layer_norm(DN_ALPHA * x + y, ln1_g[i], ln1_b[i])
        j = i // 2
        if i % 2 == 0:
            f = swiglu(x, ffn_w1[j], ffn_w3[j], ffn_w2[j])
        else:
            f = moe(x, moe_w_r[j], moe_w1[j], moe_w3[j], moe_w2[j])
        x = layer_norm(DN_ALPHA * x + f, ln2_g[i], ln2_b[i])
        gate = jax.nn.sigmoid(x @ ple_w_g[i] + ple_b_g[i])
        x = x + gate * (p[i] @ ple_w_p[i])
    return x
```

```python
import functools
import math

import jax
import jax.numpy as jnp
from jax import lax
from jax.experimental import pallas as pl
from jax.experimental.pallas import tpu as pltpu

F32 = jnp.float32
BF16 = jnp.bfloat16

D_MODEL = 1024
DEPTH = 4
D_PLE = 256
N_SG = 8
SG_DIM = 64
D_SG = N_SG * SG_DIM
CHUNK = 128
N_HEADS = 8
QK_NOPE = 64
QK_ROPE = 32
V_DIM = 64
Q_LORA = 256
KV_LORA = 128
D_ATT = N_HEADS * V_DIM
ROPE_THETA = 10000.0
D_FF = 2816
N_EXPERTS = 8
D_FF_EXPERT = 3584
DN_ALPHA = (2.0 * DEPTH) ** 0.25
EPS = 1e-6
ATT_SCALE = (QK_NOPE + QK_ROPE) ** -0.5

LANES = 128
SUBLANES = 8
HEAD_PAD = LANES
D_QK_PAD = N_HEADS * HEAD_PAD
ROW_TILES = D_MODEL // LANES
C_Q0 = 2 * D_SG
C_KV0 = C_Q0 + Q_LORA
C_KR0 = C_KV0 + KV_LORA
D_IN_PAD = C_KR0 + LANES
ROPE_LANE0 = QK_NOPE
ROPE_HALF = QK_ROPE // 2

TM = 512
TQ = 512
TF_DENSE = 256
TF_MOE = 512
TM_MOE = 512
VMEM_LIMIT = 56 * 1024 * 1024


def _params(*sem):
    return pltpu.CompilerParams(dimension_semantics=sem, vmem_limit_bytes=VMEM_LIMIT)


def _const_spec(shape):
    nd = len(shape)
    return pl.BlockSpec(shape, lambda *_: (0,) * nd, pipeline_mode=pl.Buffered(1))


def _row_spec(tm, width):
    return pl.BlockSpec((tm, width), lambda i: (i, 0))


def _layer_norm(x, g, b):
    mu = jnp.mean(x, axis=-1, keepdims=True)
    xc = x - mu
    var = jnp.mean(xc * xc, axis=-1, keepdims=True)
    return xc * lax.rsqrt(var + EPS) * g + b


def _rms_norm(x, g):
    return x * lax.rsqrt(jnp.mean(x * x, axis=-1, keepdims=True) + EPS) * g


def _gelu(x):
    return 0.5 * x * (1.0 + lax.erf(x * (1.0 / math.sqrt(2.0))))


def _silu(x):
    return x * jax.nn.sigmoid(x)


def _rope_tile(blk, rc, rs, up):
    nxt = pltpu.roll(blk, LANES - ROPE_HALF, 1)
    prv = pltpu.roll(blk, ROPE_HALF, 1)
    return blk * rc + jnp.where(up, nxt, prv) * rs


def _proj_body(x_ref, win_ref, vg_ref, vb_ref, ws_ref, bexp_ref, qg_ref, kvg_ref,
               wuq_ref, wuk_ref, wuv_ref, rc_ref, rs_ref, og_ref,
               a_ref, q_ref, k_ref, v_ref):
    tm = x_ref.shape[0]
    h = jnp.dot(x_ref[...].astype(BF16), win_ref[...], preferred_element_type=F32)

    uv = _gelu(h[:, :C_Q0])
    u = uv[:, :D_SG]
    v = _layer_norm(uv[:, D_SG:], vg_ref[...], vb_ref[...]).astype(BF16)
    t_idx = lax.broadcasted_iota(jnp.int32, (CHUNK, CHUNK), 0)
    s_idx = lax.broadcasted_iota(jnp.int32, (CHUNK, CHUNK), 1)
    causal = s_idx <= t_idx
    first_group = s_idx < SG_DIM
    wpair = [
        jnp.concatenate([jnp.where(causal, ws_ref[2 * j], 0.0),
                         jnp.where(causal, ws_ref[2 * j + 1], 0.0)], axis=0).astype(BF16)
        for j in range(N_SG // 2)
    ]
    rows = []
    for c in range(tm // CHUNK):
        blks = []
        for j in range(N_SG // 2):
            vblk = v[c * CHUNK:(c + 1) * CHUNK, j * LANES:(j + 1) * LANES]
            mm = jnp.dot(wpair[j], vblk, preferred_element_type=F32)
            blks.append(jnp.where(first_group, mm[:CHUNK], mm[CHUNK:]))
        rows.append(jnp.concatenate(blks, axis=1) + bexp_ref[...])
    mixed = jnp.concatenate(rows, axis=0)
    a_ref[...] = _rms_norm(u * mixed, og_ref[...]).astype(BF16)

    rc = rc_ref[...]
    rs = rs_ref[...]
    up = lax.broadcasted_iota(jnp.int32, (tm, LANES), 1) < ROPE_LANE0 + ROPE_HALF

    cq = _rms_norm(h[:, C_Q0:C_KV0], qg_ref[...]).astype(BF16)
    q = jnp.dot(cq, wuq_ref[...], preferred_element_type=F32)
    for hd in range(N_HEADS):
        sl = slice(hd * HEAD_PAD, (hd + 1) * HEAD_PAD)
        q_ref[:, sl] = (_rope_tile(q[:, sl], rc, rs, up) * ATT_SCALE).astype(BF16)

    ckv = _rms_norm(h[:, C_KV0:C_KR0], kvg_ref[...]).astype(BF16)
    kn = jnp.dot(ckv, wuk_ref[...], preferred_element_type=F32)
    kr = _rope_tile(h[:, C_KR0:], rc, rs, up)
    for hd in range(N_HEADS):
        sl = slice(hd * HEAD_PAD, (hd + 1) * HEAD_PAD)
        k_ref[:, sl] = (kn[:, sl] + kr).astype(BF16)
    v_ref[...] = jnp.dot(ckv, wuv_ref[...], preferred_element_type=F32).astype(BF16)


def _proj_call(x, win, vg, vb, ws, bexp, qg, kvg, wuq, wuk, wuv, rc, rs, og):
    n = x.shape[0]
    return pl.pallas_call(
        _proj_body,
        grid=(n // TM,),
        in_specs=[
            _row_spec(TM, D_MODEL),
            _const_spec(win.shape), _const_spec(vg.shape), _const_spec(vb.shape),
            _const_spec(ws.shape), _const_spec(bexp.shape), _const_spec(qg.shape),
            _const_spec(kvg.shape), _const_spec(wuq.shape), _const_spec(wuk.shape),
            _const_spec(wuv.shape),
            _row_spec(TM, LANES), _row_spec(TM, LANES),
            _const_spec(og.shape),
        ],
        out_specs=[_row_spec(TM, D_SG), _row_spec(TM, D_QK_PAD), _row_spec(TM, D_QK_PAD),
                   _row_spec(TM, D_ATT)],
        out_shape=[jax.ShapeDtypeStruct((n, D_SG), BF16),
                   jax.ShapeDtypeStruct((n, D_QK_PAD), BF16),
                   jax.ShapeDtypeStruct((n, D_QK_PAD), BF16),
                   jax.ShapeDtypeStruct((n, D_ATT), BF16)],
        compiler_params=_params("parallel"),
        name="proj",
    )(x, win, vg, vb, ws, bexp, qg, kvg, wuq, wuk, wuv, rc, rs, og)


def _attn_body(q_ref, k_ref, v_ref, o_ref, m_sc, l_sc, acc_sc):
    tq = q_ref.shape[1]
    qi = pl.program_id(2)
    m_sc[...] = jnp.full(m_sc.shape, -jnp.inf, F32)
    l_sc[...] = jnp.zeros(l_sc.shape, F32)
    acc_sc[...] = jnp.zeros(acc_sc.shape, F32)

    def step(off, mask):
        vblk = v_ref[0, pl.ds(off, tq), :]
        for hh in range(2):
            sl = slice(hh * HEAD_PAD, (hh + 1) * HEAD_PAD)
            s = lax.dot_general(q_ref[0, :, sl], k_ref[0, pl.ds(off, tq), sl],
                                (((1,), (1,)), ((), ())), preferred_element_type=F32)
            if mask is not None:
                s = jnp.where(mask, s, -jnp.inf)
            m_prev = m_sc[hh]
            m_next = jnp.maximum(m_prev, jnp.max(s, axis=1, keepdims=True))
            p = jnp.exp(s - jnp.concatenate([m_next] * (tq // LANES), axis=1))
            alpha = jnp.exp(m_prev - m_next)
            l_sc[hh] = alpha * l_sc[hh] + jnp.sum(p, axis=1, keepdims=True)
            acc_sc[hh] = alpha * acc_sc[hh] + jnp.dot(p.astype(BF16), vblk,
                                                      preferred_element_type=F32)
            m_sc[hh] = m_next

    def full_block(ki, carry):
        step(pl.multiple_of(ki * tq, tq), None)
        return carry

    lax.fori_loop(0, qi, full_block, 0)
    row = lax.broadcasted_iota(jnp.int32, (tq, tq), 0)
    col = lax.broadcasted_iota(jnp.int32, (tq, tq), 1)
    step(pl.multiple_of(qi * tq, tq), col <= row)

    first_head = lax.broadcasted_iota(jnp.int32, (tq, LANES), 1) < V_DIM
    o_ref[0] = jnp.where(first_head, acc_sc[0] / l_sc[0], acc_sc[1] / l_sc[1])


def _attn_call(q, k, v):
    b, s, _ = q.shape
    return pl.pallas_call(
        _attn_body,
        grid=(b, N_HEADS // 2, s // TQ),
        in_specs=[
            pl.BlockSpec((1, TQ, 2 * HEAD_PAD), lambda bi, pi, qi: (bi, qi, pi)),
            pl.BlockSpec((1, s, 2 * HEAD_PAD), lambda bi, pi, qi: (bi, 0, pi)),
            pl.BlockSpec((1, s, 2 * V_DIM), lambda bi, pi, qi: (bi, 0, pi)),
        ],
        out_specs=pl.BlockSpec((1, TQ, 2 * V_DIM), lambda bi, pi, qi: (bi, qi, pi)),
        out_shape=jax.ShapeDtypeStruct((b, s, D_ATT), F32),
        scratch_shapes=[pltpu.VMEM((2, TQ, LANES), F32)] * 3,
        compiler_params=_params("parallel", "parallel", "arbitrary"),
        name="attn",
    )(q, k, v)


def _mix_out(a_ref, m_ref, x_ref, ogm_ref, wo_ref, g_ref, b_ref):
    mn = _rms_norm(m_ref[...], ogm_ref[...]).astype(BF16)
    y = jnp.dot(a_ref[...], wo_ref[:D_SG, :], preferred_element_type=F32)
    y = y + jnp.dot(mn, wo_ref[D_SG:, :], preferred_element_type=F32)
    return _layer_norm(DN_ALPHA * x_ref[...] + y, g_ref[...], b_ref[...])


def _oproj_body(a_ref, m_ref, x_ref, ogm_ref, wo_ref, g_ref, b_ref, x1_ref):
    x1_ref[...] = _mix_out(a_ref, m_ref, x_ref, ogm_ref, wo_ref, g_ref, b_ref)


def _store_row_tiles(dst_ref, val):
    tm = val.shape[0]
    for j in range(ROW_TILES):
        dst_ref[pl.ds(j, tm, stride=ROW_TILES), :] = val[:, j * LANES:(j + 1) * LANES]


def _load_row_tiles(src_ref, tm):
    return jnp.concatenate(
        [src_ref[pl.ds(j, tm, stride=ROW_TILES), :] for j in range(ROW_TILES)], axis=1)


def _oproj_route_body(a_ref, m_ref, x_ref, ogm_ref, wo_ref, g_ref, b_ref, wr_ref,
                      x1_ref, x1t_ref, route_ref):
    x1 = _mix_out(a_ref, m_ref, x_ref, ogm_ref, wo_ref, g_ref, b_ref)
    x1_ref[...] = x1
    _store_row_tiles(x1t_ref, x1)
    tm = x1.shape[0]
    lane = lax.broadcasted_iota(jnp.int32, (tm, LANES), 1)
    logits = jnp.dot(x1, wr_ref[...], preferred_element_type=F32,
                     precision=lax.Precision.HIGHEST)
    lg = jnp.where(lane < N_EXPERTS, logits, -jnp.inf)
    m1 = jnp.max(lg, axis=1, keepdims=True)
    i1 = jnp.min(jnp.where(lg == m1, lane, LANES), axis=1, keepdims=True)
    lg2 = jnp.where(lane == i1, -jnp.inf, lg)
    m2 = jnp.max(lg2, axis=1, keepdims=True)
    i2 = jnp.min(jnp.where(lg2 == m2, lane, LANES), axis=1, keepdims=True)
    e = jnp.exp(m2 - m1)
    w1 = 1.0 / (1.0 + e)
    w2 = e / (1.0 + e)
    first_lo = i1 < i2
    e_lo = jnp.where(first_lo, i1, i2).astype(F32)
    e_hi = jnp.where(first_lo, i2, i1).astype(F32)
    w_lo = jnp.where(first_lo, w1, w2)
    w_hi = jnp.where(first_lo, w2, w1)
    route_ref[...] = jnp.where(lane == 0, e_lo, jnp.where(lane == 1, e_hi,
                               jnp.where(lane == 2, w_lo, jnp.where(lane == 3, w_hi, 0.0))))


def _oproj_call(a, m, x, ogm, wo, g, b, wr=None):
    n = x.shape[0]
    in_specs = [_row_spec(TM, D_SG), _row_spec(TM, D_ATT), _row_spec(TM, D_MODEL),
                _const_spec(ogm.shape), _const_spec(wo.shape), _const_spec(g.shape),
                _const_spec(b.shape)]
    if wr is None:
        return pl.pallas_call(
            _oproj_body, grid=(n // TM,), in_specs=in_specs,
            out_specs=_row_spec(TM, D_MODEL),
            out_shape=jax.ShapeDtypeStruct((n, D_MODEL), F32),
            compiler_params=_params("parallel"), name="oproj",
        )(a, m, x, ogm, wo, g, b)
    return pl.pallas_call(
        _oproj_route_body, grid=(n // TM,), in_specs=in_specs + [_const_spec(wr.shape)],
        out_specs=[_row_spec(TM, D_MODEL), _row_spec(TM * ROW_TILES, LANES),
                   _row_spec(TM, LANES)],
        out_shape=[jax.ShapeDtypeStruct((n, D_MODEL), F32),
                   jax.ShapeDtypeStruct((n * ROW_TILES, LANES), F32),
                   jax.ShapeDtypeStruct((n, LANES), F32)],
        compiler_params=_params("parallel"), name="oproj_route",
    )(a, m, x, ogm, wo, g, b, wr)


def _ffn_body(x_ref, w1_ref, w3_ref, w2_ref, f_ref):
    xb = x_ref[...].astype(BF16)
    acc = jnp.zeros(f_ref.shape, F32)
    for c in range(D_FF // TF_DENSE):
        sl = slice(c * TF_DENSE, (c + 1) * TF_DENSE)
        g = jnp.dot(xb, w1_ref[:, sl], preferred_element_type=F32)
        u = jnp.dot(xb, w3_ref[:, sl], preferred_element_type=F32)
        acc = acc + jnp.dot((_silu(g) * u).astype(BF16), w2_ref[sl, :],
                            preferred_element_type=F32)
    f_ref[...] = acc


def _ffn_call(x1, w1, w3, w2):
    n = x1.shape[0]
    return pl.pallas_call(
        _ffn_body, grid=(n // TM,),
        in_specs=[_row_spec(TM, D_MODEL), _const_spec(w1.shape), _const_spec(w3.shape),
                  _const_spec(w2.shape)],
        out_specs=_row_spec(TM, D_MODEL),
        out_shape=jax.ShapeDtypeStruct((n, D_MODEL), F32),
        compiler_params=_params("parallel"), name="ffn_dense",
    )(x1, w1, w3, w2)


def _dispatch_body(dest_ref, x_ref, init_ref, xs_ref, sem):
    del init_ref
    tm = x_ref.shape[0]

    def issue(t, carry):
        for slot in range(2):
            pltpu.make_async_copy(x_ref.at[t], xs_ref.at[dest_ref[0, 0, 2 * t + slot]],
                                  sem).start()
        return carry

    lax.fori_loop(0, tm, issue, 0)
    for slot in range(2):
        pltpu.make_async_copy(x_ref, xs_ref.at[pl.ds(0, tm)], sem).wait()


def _dispatch_call(dest, x1t, xs_init):
    n = x1t.shape[0]
    return pl.pallas_call(
        _dispatch_body, grid=(n // TM,),
        in_specs=[pl.BlockSpec((1, 1, 2 * TM), lambda i: (i, 0, 0), memory_space=pltpu.SMEM),
                  pl.BlockSpec((TM, ROW_TILES, LANES), lambda i: (i, 0, 0)),
                  pl.BlockSpec(memory_space=pl.ANY)],
        out_specs=pl.BlockSpec(memory_space=pl.ANY),
        out_shape=jax.ShapeDtypeStruct(xs_init.shape, F32),
        scratch_shapes=[pltpu.SemaphoreType.DMA],
        input_output_aliases={2: 0},
        compiler_params=_params("arbitrary"), name="moe_dispatch",
    )(dest, x1t, xs_init)


def _moe_body(te_ref, tv_ref, xs_ref, w1_ref, w3_ref, w2_ref, ys_ref, xb_sc, acc_sc):
    t = pl.program_id(0)
    j = pl.program_id(1)
    tm = xb_sc.shape[0]

    @pl.when(tv_ref[t] == 1)
    def _():
        @pl.when(j == 0)
        def _():
            xb_sc[...] = _load_row_tiles(xs_ref, tm).astype(BF16)
            acc_sc[...] = jnp.zeros(acc_sc.shape, F32)

        xb = xb_sc[...]
        g = jnp.dot(xb, w1_ref[0], preferred_element_type=F32)
        u = jnp.dot(xb, w3_ref[0], preferred_element_type=F32)
        acc_sc[...] += jnp.dot((_silu(g) * u).astype(BF16), w2_ref[0],
                               preferred_element_type=F32)

    @pl.when(j == pl.num_programs(1) - 1)
    def _():
        @pl.when(tv_ref[t] == 1)
        def _():
            _store_row_tiles(ys_ref, acc_sc[...])

        @pl.when(tv_ref[t] == 0)
        def _():
            ys_ref[...] = jnp.zeros(ys_ref.shape, F32)


def _moe_call(tile_expert, tile_valid, xs, w1, w3, w2):
    n_tiles = xs.shape[0] // (TM_MOE * ROW_TILES)
    n_ff = D_FF_EXPERT // TF_MOE

    def ff_idx(t, j, te, tv):
        return j * tv[t] + (n_ff - 1) * (1 - tv[t])

    return pl.pallas_call(
        _moe_body,
        grid_spec=pltpu.PrefetchScalarGridSpec(
            num_scalar_prefetch=2,
            grid=(n_tiles, n_ff),
            in_specs=[
                pl.BlockSpec((TM_MOE * ROW_TILES, LANES), lambda t, j, te, tv: (t, 0)),
                pl.BlockSpec((1, D_MODEL, TF_MOE),
                             lambda t, j, te, tv: (te[t], 0, ff_idx(t, j, te, tv))),
                pl.BlockSpec((1, D_MODEL, TF_MOE),
                             lambda t, j, te, tv: (te[t], 0, ff_idx(t, j, te, tv))),
                pl.BlockSpec((1, TF_MOE, D_MODEL),
                             lambda t, j, te, tv: (te[t], ff_idx(t, j, te, tv), 0)),
            ],
            out_specs=pl.BlockSpec((TM_MOE * ROW_TILES, LANES), lambda t, j, te, tv: (t, 0)),
            scratch_shapes=[pltpu.VMEM((TM_MOE, D_MODEL), BF16),
                            pltpu.VMEM((TM_MOE, D_MODEL), F32)],
        ),
        out_shape=jax.ShapeDtypeStruct(xs.shape, F32),
        compiler_params=_params("arbitrary", "arbitrary"), name="moe_experts",
    )(tile_expert, tile_valid, xs, w1, w3, w2)


def _combine_body(dest_ref, route_ref, ys_ref, f_ref, buf, sem):
    tm = f_ref.shape[0]

    def issue(t, carry):
        for slot in range(2):
            pltpu.make_async_copy(ys_ref.at[dest_ref[0, 0, 2 * t + slot]], buf.at[slot, t],
                                  sem).start()
        return carry

    lax.fori_loop(0, tm, issue, 0)
    for slot in range(2):
        pltpu.make_async_copy(ys_ref.at[pl.ds(0, tm)], buf.at[slot], sem).wait()
    route = route_ref[...]
    w_lo = route[:, 2:3]
    w_hi = route[:, 3:4]
    f_ref[...] = jnp.concatenate(
        [w_lo * buf[0, :, j, :] + w_hi * buf[1, :, j, :] for j in range(ROW_TILES)], axis=1)


def _combine_call(dest, route, ys):
    n = route.shape[0]
    return pl.pallas_call(
        _combine_body, grid=(n // TM,),
        in_specs=[pl.BlockSpec((1, 1, 2 * TM), lambda i: (i, 0, 0), memory_space=pltpu.SMEM),
                  _row_spec(TM, LANES),
                  pl.BlockSpec(memory_space=pl.ANY)],
        out_specs=_row_spec(TM, D_MODEL),
        out_shape=jax.ShapeDtypeStruct((n, D_MODEL), F32),
        scratch_shapes=[pltpu.VMEM((2, TM, ROW_TILES, LANES), F32), pltpu.SemaphoreType.DMA],
        compiler_params=_params("arbitrary"), name="moe_combine",
    )(dest, route, ys)


def _route_plan(route):
    n = route.shape[0]
    e_pair = route[:, :2].astype(jnp.int32)
    onehot = (e_pair[:, :, None] == jnp.arange(N_EXPERTS)[None, None, :]).any(axis=1)
    onehot = onehot.astype(jnp.int32)
    rank = jnp.cumsum(onehot, axis=0) - onehot
    counts = jnp.sum(onehot, axis=0)
    tiles = (counts + TM_MOE - 1) // TM_MOE
    tile_end = jnp.cumsum(tiles)
    row0 = (tile_end - tiles) * TM_MOE
    dest = jnp.take_along_axis(row0[None, :] + rank, e_pair, axis=1)
    n_tiles = 2 * n // TM_MOE + N_EXPERTS
    t_idx = jnp.arange(n_tiles)
    tile_expert = jnp.minimum(jnp.sum(t_idx[:, None] >= tile_end[None, :], axis=1),
                              N_EXPERTS - 1).astype(jnp.int32)
    tile_valid = (t_idx < tile_end[-1]).astype(jnp.int32)
    last_expert = jnp.max(jnp.where(tiles > 0, jnp.arange(N_EXPERTS), 0)).astype(jnp.int32)
    tile_expert = jnp.where(tile_valid == 1, tile_expert, last_expert)
    return dest.astype(jnp.int32).reshape(n // TM, 1, 2 * TM), tile_expert, tile_valid, n_tiles


def _ple_body(x1_ref, f_ref, p_ref, g_ref, b_ref, wg_ref, bg_ref, wp_ref, o_ref):
    x2 = _layer_norm(DN_ALPHA * x1_ref[...] + f_ref[...], g_ref[...], b_ref[...])
    gate = jax.nn.sigmoid(
        jnp.dot(x2.astype(BF16), wg_ref[...], preferred_element_type=F32) + bg_ref[...])
    emb = jnp.dot(p_ref[...].astype(BF16), wp_ref[...], preferred_element_type=F32)
    o_ref[...] = x2 + gate * emb


def _ple_call(x1, f, p, g, b, wg, bg, wp):
    n = x1.shape[0]
    return pl.pallas_call(
        _ple_body, grid=(n // TM,),
        in_specs=[_row_spec(TM, D_MODEL), _row_spec(TM, D_MODEL), _row_spec(TM, D_PLE),
                  _const_spec(g.shape), _const_spec(b.shape), _const_spec(wg.shape),
                  _const_spec(bg.shape), _const_spec(wp.shape)],
        out_specs=_row_spec(TM, D_MODEL),
        out_shape=jax.ShapeDtypeStruct((n, D_MODEL), F32),
        compiler_params=_params("parallel"), name="ln_ple",
    )(x1, f, p, g, b, wg, bg, wp)


def _rope_tables(positions):
    inv_freq = 1.0 / (ROPE_THETA ** (jnp.arange(0, QK_ROPE, 2, dtype=F32) / QK_ROPE))
    ang = positions.astype(F32).reshape(-1, 1) * inv_freq
    cos, sin = jnp.cos(ang), jnp.sin(ang)
    n = cos.shape[0]
    ones = jnp.ones((n, ROPE_LANE0), F32)
    zeros = jnp.zeros((n, ROPE_LANE0), F32)
    tail1 = jnp.ones((n, LANES - ROPE_LANE0 - QK_ROPE), F32)
    tail0 = jnp.zeros((n, LANES - ROPE_LANE0 - QK_ROPE), F32)
    rc = jnp.concatenate([ones, cos, cos, tail1], axis=1)
    rs = jnp.concatenate([zeros, -sin, sin, tail0], axis=1)
    return rc, rs


def _row(v):
    return v.reshape(1, -1).astype(F32)


def kernel(x, p, positions, w_in, sg_v_g, sg_v_b, sg_w_s, sg_b_s, q_norm_g, kv_norm_g, w_uq, w_ukv, out_g, w_o, ln1_g, ln1_b, ffn_w1, ffn_w3, ffn_w2, moe_w_r, moe_w1, moe_w3, moe_w2, ln2_g, ln2_b, ple_w_g, ple_b_g, ple_w_p):
    bsz, seq, _ = x.shape
    n = bsz * seq
    xf = x.reshape(n, D_MODEL)
    rc, rs = _rope_tables(positions)

    for i in range(DEPTH):
        win = jnp.concatenate(
            [w_in[i, :, :C_KR0], jnp.zeros((D_MODEL, ROPE_LANE0), F32), w_in[i, :, C_KR0:],
             jnp.zeros((D_MODEL, LANES - ROPE_LANE0 - QK_ROPE), F32)], axis=1).astype(BF16)
        wuq = jnp.pad(w_uq[i].reshape(Q_LORA, N_HEADS, QK_NOPE + QK_ROPE),
                      ((0, 0), (0, 0), (0, HEAD_PAD - QK_NOPE - QK_ROPE)))
        wuq = wuq.reshape(Q_LORA, D_QK_PAD).astype(BF16)
        wukv = w_ukv[i].reshape(KV_LORA, N_HEADS, QK_NOPE + V_DIM)
        wuk = jnp.pad(wukv[:, :, :QK_NOPE], ((0, 0), (0, 0), (0, HEAD_PAD - QK_NOPE)))
        wuk = wuk.reshape(KV_LORA, D_QK_PAD).astype(BF16)
        wuv = wukv[:, :, QK_NOPE:].reshape(KV_LORA, D_ATT).astype(BF16)
        bexp = jnp.repeat(sg_b_s[i].T, SG_DIM, axis=1)
        a, q, k, v = _proj_call(xf, win, _row(sg_v_g[i]), _row(sg_v_b[i]), sg_w_s[i], bexp,
                                _row(q_norm_g[i]), _row(kv_norm_g[i]), wuq, wuk, wuv, rc, rs,
                                _row(out_g[i, :D_SG]))
        m = _attn_call(q.reshape(bsz, seq, D_QK_PAD), k.reshape(bsz, seq, D_QK_PAD),
                       v.reshape(bsz, seq, D_ATT)).reshape(n, D_ATT)

        j = i // 2
        mix_args = (a, m, xf, _row(out_g[i, D_SG:]), w_o[i].astype(BF16), _row(ln1_g[i]),
                    _row(ln1_b[i]))
        if i % 2 == 0:
            x1 = _oproj_call(*mix_args)
            f = _ffn_call(x1, ffn_w1[j].astype(BF16), ffn_w3[j].astype(BF16),
                          ffn_w2[j].astype(BF16))
        else:
            wr = jnp.pad(moe_w_r[j], ((0, 0), (0, LANES - N_EXPERTS)))
            x1, x1t, route = _oproj_call(*mix_args, wr=wr)
            dest, tile_expert, tile_valid, n_tiles = _route_plan(route)
            xs0 = jnp.zeros((n_tiles * TM_MOE, ROW_TILES, LANES), F32)
            xs = _dispatch_call(dest, x1t.reshape(n, ROW_TILES, LANES), xs0)
            ys = _moe_call(tile_expert, tile_valid, xs.reshape(-1, LANES),
                           moe_w1[j].astype(BF16), moe_w3[j].astype(BF16),
                           moe_w2[j].astype(BF16))
            f = _combine_call(dest, route, ys.reshape(-1, ROW_TILES, LANES))

        xf = _ple_call(x1, f, p[i].reshape(n, D_PLE), _row(ln2_g[i]), _row(ln2_b[i]),
                       ple_w_g[i].astype(BF16), _row(ple_b_g[i]), ple_w_p[i].astype(BF16))

    return xf.reshape(bsz, seq, D_MODEL)
```

```python
import functools
import math

import jax
import jax.numpy as jnp
from jax import lax
from jax.experimental import pallas as pl
from jax.experimental.pallas import tpu as pltpu

F32 = jnp.float32
BF16 = jnp.bfloat16

D_MODEL = 1024
DEPTH = 4
D_PLE = 256
N_SG = 8
SG_DIM = 64
D_SG = N_SG * SG_DIM
CHUNK = 128
N_HEADS = 8
QK_NOPE = 64
QK_ROPE = 32
V_DIM = 64
Q_LORA = 256
KV_LORA = 128
D_ATT = N_HEADS * V_DIM
ROPE_THETA = 10000.0
D_FF = 2816
N_EXPERTS = 8
D_FF_EXPERT = 3584
DN_ALPHA = (2.0 * DEPTH) ** 0.25
EPS = 1e-6
ATT_SCALE = (QK_NOPE + QK_ROPE) ** -0.5 * math.log2(math.e)

LANES = 128
SUBLANES = 8
HEAD_PAD = LANES
D_QK_PAD = N_HEADS * HEAD_PAD
ROW_TILES = D_MODEL // LANES
C_Q0 = 2 * D_SG
C_KV0 = C_Q0 + Q_LORA
C_KR0 = C_KV0 + KV_LORA
D_IN_PAD = C_KR0 + LANES
ROPE_LANE0 = QK_NOPE
ROPE_HALF = QK_ROPE // 2

TM = 512
TQ = 512
TF_DENSE = 256
TF_MOE = 512
TM_MOE = 512
VMEM_LIMIT = 56 * 1024 * 1024


def _params(*sem):
    return pltpu.CompilerParams(dimension_semantics=sem, vmem_limit_bytes=VMEM_LIMIT)


def _const_spec(shape):
    nd = len(shape)
    return pl.BlockSpec(shape, lambda *_: (0,) * nd, pipeline_mode=pl.Buffered(1))


def _row_spec(tm, width):
    return pl.BlockSpec((tm, width), lambda i: (i, 0))


def _layer_norm(x, g, b):
    mu = jnp.mean(x, axis=-1, keepdims=True)
    xc = x - mu
    var = jnp.mean(xc * xc, axis=-1, keepdims=True)
    return xc * lax.rsqrt(var + EPS) * g + b


def _rms_norm(x, g):
    return x * lax.rsqrt(jnp.mean(x * x, axis=-1, keepdims=True) + EPS) * g


def _gelu(x):
    return 0.5 * x * (1.0 + lax.erf(x * (1.0 / math.sqrt(2.0))))


def _silu(x):
    return x * jax.nn.sigmoid(x)


def _rope_tile(blk, rc, rs, up):
    nxt = pltpu.roll(blk, LANES - ROPE_HALF, 1)
    prv = pltpu.roll(blk, ROPE_HALF, 1)
    return blk * rc + jnp.where(up, nxt, prv) * rs


def _proj_body(x_ref, win_ref, vg_ref, vb_ref, ws_ref, bexp_ref, qg_ref, kvg_ref,
               wuq_ref, wuk_ref, wuv_ref, rc_ref, rs_ref, og_ref,
               a_ref, q_ref, k_ref, v_ref):
    tm = x_ref.shape[0]
    h = jnp.dot(x_ref[...].astype(BF16), win_ref[...], preferred_element_type=F32)

    uv = _gelu(h[:, :C_Q0])
    u = uv[:, :D_SG]
    v = _layer_norm(uv[:, D_SG:], vg_ref[...], vb_ref[...]).astype(BF16)
    t_idx = lax.broadcasted_iota(jnp.int32, (CHUNK, CHUNK), 0)
    s_idx = lax.broadcasted_iota(jnp.int32, (CHUNK, CHUNK), 1)
    causal = s_idx <= t_idx
    first_group = s_idx < SG_DIM
    wpair = [
        jnp.concatenate([jnp.where(causal, ws_ref[2 * j], 0.0),
                         jnp.where(causal, ws_ref[2 * j + 1], 0.0)], axis=0).astype(BF16)
        for j in range(N_SG // 2)
    ]
    rows = []
    for c in range(tm // CHUNK):
        blks = []
        for j in range(N_SG // 2):
            vblk = v[c * CHUNK:(c + 1) * CHUNK, j * LANES:(j + 1) * LANES]
            mm = jnp.dot(wpair[j], vblk, preferred_element_type=F32)
            blks.append(jnp.where(first_group, mm[:CHUNK], mm[CHUNK:]))
        rows.append(jnp.concatenate(blks, axis=1) + bexp_ref[...])
    mixed = jnp.concatenate(rows, axis=0)
    a_ref[...] = _rms_norm(u * mixed, og_ref[...]).astype(BF16)

    rc = rc_ref[...]
    rs = rs_ref[...]
    up = lax.broadcasted_iota(jnp.int32, (tm, LANES), 1) < ROPE_LANE0 + ROPE_HALF

    cq = _rms_norm(h[:, C_Q0:C_KV0], qg_ref[...]).astype(BF16)
    q = jnp.dot(cq, wuq_ref[...], preferred_element_type=F32)
    for hd in range(N_HEADS):
        sl = slice(hd * HEAD_PAD, (hd + 1) * HEAD_PAD)
        q_ref[:, sl] = (_rope_tile(q[:, sl], rc, rs, up) * ATT_SCALE).astype(BF16)

    ckv = _rms_norm(h[:, C_KV0:C_KR0], kvg_ref[...]).astype(BF16)
    kn = jnp.dot(ckv, wuk_ref[...], preferred_element_type=F32)
    kr = _rope_tile(h[:, C_KR0:], rc, rs, up)
    for hd in range(N_HEADS):
        sl = slice(hd * HEAD_PAD, (hd + 1) * HEAD_PAD)
        k_ref[:, sl] = (kn[:, sl] + kr).astype(BF16)
    vv = jnp.dot(ckv, wuv_ref[...], preferred_element_type=F32)
    value_lanes = lax.broadcasted_iota(jnp.int32, (tm, LANES), 1) < V_DIM
    for hd in range(N_HEADS):
        sl = slice(hd * HEAD_PAD, (hd + 1) * HEAD_PAD)
        v_ref[:, sl] = jnp.where(value_lanes, vv[:, sl], 1.0).astype(BF16)


def _proj_call(x, win, vg, vb, ws, bexp, qg, kvg, wuq, wuk, wuv, rc, rs, og):
    n = x.shape[0]
    return pl.pallas_call(
        _proj_body,
        grid=(n // TM,),
        in_specs=[
            _row_spec(TM, D_MODEL),
            _const_spec(win.shape), _const_spec(vg.shape), _const_spec(vb.shape),
            _const_spec(ws.shape), _const_spec(bexp.shape), _const_spec(qg.shape),
            _const_spec(kvg.shape), _const_spec(wuq.shape), _const_spec(wuk.shape),
            _const_spec(wuv.shape),
            _row_spec(TM, LANES), _row_spec(TM, LANES),
            _const_spec(og.shape),
        ],
        out_specs=[_row_spec(TM, D_SG)] + [_row_spec(TM, D_QK_PAD)] * 3,
        out_shape=[jax.ShapeDtypeStruct((n, D_SG), BF16)]
        + [jax.ShapeDtypeStruct((n, D_QK_PAD), BF16)] * 3,
        compiler_params=_params("parallel"),
        name="proj",
    )(x, win, vg, vb, ws, bexp, qg, kvg, wuq, wuk, wuv, rc, rs, og)


def _attn_body(q_ref, k_ref, v_ref, o_ref, m_sc, acc_sc):
    tq = q_ref.shape[1]
    qi = pl.program_id(1)

    def step(off, mask):
        def scores(hh):
            sl = slice(hh * HEAD_PAD, (hh + 1) * HEAD_PAD)
            return lax.dot_general(q_ref[0, :, sl], k_ref[0, pl.ds(off, tq), sl],
                                   (((1,), (1,)), ((), ())), preferred_element_type=F32)

        ahead = 2
        s_queue = [scores(hh) for hh in range(ahead)]
        for hh in range(N_HEADS):
            sl = slice(hh * HEAD_PAD, (hh + 1) * HEAD_PAD)
            s = s_queue.pop(0)
            if hh + ahead < N_HEADS:
                s_queue.append(scores(hh + ahead))
            vblk = v_ref[0, pl.ds(off, tq), sl]
            if mask is not None:
                s = jnp.where(mask, s, -jnp.inf)
                m_next = jnp.broadcast_to(jnp.max(s, axis=1, keepdims=True), (tq, LANES))
                p = jnp.exp2(s - jnp.concatenate([m_next] * (tq // LANES), axis=1)).astype(BF16)
                acc_sc[hh] = jnp.dot(p, vblk, preferred_element_type=F32)
            else:
                m_prev = m_sc[hh]
                m_next = jnp.maximum(m_prev, jnp.max(s, axis=1, keepdims=True))
                p = jnp.exp2(s - jnp.concatenate([m_next] * (tq // LANES), axis=1)).astype(BF16)
                acc_sc[hh] = jnp.exp2(m_prev - m_next) * acc_sc[hh] + jnp.dot(
                    p, vblk, preferred_element_type=F32)
            m_sc[hh] = m_next

    row = lax.broadcasted_iota(jnp.int32, (tq, tq), 0)
    col = lax.broadcasted_iota(jnp.int32, (tq, tq), 1)
    step(pl.multiple_of(qi * tq, tq), col <= row)

    def full_block(ki, carry):
        step(pl.multiple_of(ki * tq, tq), None)
        return carry

    lax.fori_loop(0, qi, full_block, 0)

    value_lanes = lax.broadcasted_iota(jnp.int32, (tq, LANES), 1) < V_DIM
    outs = []
    for hh in range(N_HEADS):
        acc = acc_sc[hh]
        row_sum = jnp.where(value_lanes, pltpu.roll(acc, V_DIM, 1), 1.0)
        outs.append(acc / row_sum)
    for pair in range(N_HEADS // 2):
        o_ref[0, :, pair * LANES:(pair + 1) * LANES] = jnp.where(
            value_lanes, outs[2 * pair], pltpu.roll(outs[2 * pair + 1], V_DIM, 1))


def _attn_call(q, k, v):
    b, s, _ = q.shape
    kv_spec = pl.BlockSpec((1, s, D_QK_PAD), lambda bi, qi: (bi, 0, 0))
    return pl.pallas_call(
        _attn_body,
        grid=(b, s // TQ),
        in_specs=[pl.BlockSpec((1, TQ, D_QK_PAD), lambda bi, qi: (bi, qi, 0)), kv_spec, kv_spec],
        out_specs=pl.BlockSpec((1, TQ, D_ATT), lambda bi, qi: (bi, qi, 0)),
        out_shape=jax.ShapeDtypeStruct((b, s, D_ATT), F32),
        scratch_shapes=[pltpu.VMEM((N_HEADS, TQ, LANES), F32)] * 2,
        compiler_params=_params("parallel", "arbitrary"),
        name="attn",
    )(q, k, v)


def _mix_out(a_ref, m_ref, x_ref, ogm_ref, wo_ref, g_ref, b_ref):
    mn = _rms_norm(m_ref[...], ogm_ref[...]).astype(BF16)
    y = jnp.dot(a_ref[...], wo_ref[:D_SG, :], preferred_element_type=F32)
    y = y + jnp.dot(mn, wo_ref[D_SG:, :], preferred_element_type=F32)
    return _layer_norm(DN_ALPHA * x_ref[...] + y, g_ref[...], b_ref[...])


def _oproj_body(a_ref, m_ref, x_ref, ogm_ref, wo_ref, g_ref, b_ref, x1_ref):
    x1_ref[...] = _mix_out(a_ref, m_ref, x_ref, ogm_ref, wo_ref, g_ref, b_ref)


def _store_row_tiles(dst_ref, val):
    tm = val.shape[0]
    for j in range(ROW_TILES):
        dst_ref[pl.ds(j, tm, stride=ROW_TILES), :] = val[:, j * LANES:(j + 1) * LANES]


def _load_row_tiles(src_ref, tm):
    return jnp.concatenate(
        [src_ref[pl.ds(j, tm, stride=ROW_TILES), :] for j in range(ROW_TILES)], axis=1)


def _oproj_route_body(a_ref, m_ref, x_ref, ogm_ref, wo_ref, g_ref, b_ref, wr_ref,
                      x1_ref, x1t_ref, route_ref):
    x1 = _mix_out(a_ref, m_ref, x_ref, ogm_ref, wo_ref, g_ref, b_ref)
    x1_ref[...] = x1
    _store_row_tiles(x1t_ref, x1)
    tm = x1.shape[0]
    lane = lax.broadcasted_iota(jnp.int32, (tm, LANES), 1)
    logits = jnp.dot(x1, wr_ref[...], preferred_element_type=F32,
                     precision=lax.Precision.HIGHEST)
    lg = jnp.where(lane < N_EXPERTS, logits, -jnp.inf)
    m1 = jnp.max(lg, axis=1, keepdims=True)
    i1 = jnp.min(jnp.where(lg == m1, lane, LANES), axis=1, keepdims=True)
    lg2 = jnp.where(lane == i1, -jnp.inf, lg)
    m2 = jnp.max(lg2, axis=1, keepdims=True)
    i2 = jnp.min(jnp.where(lg2 == m2, lane, LANES), axis=1, keepdims=True)
    e = jnp.exp(m2 - m1)
    w1 = 1.0 / (1.0 + e)
    w2 = e / (1.0 + e)
    first_lo = i1 < i2
    e_lo = jnp.where(first_lo, i1, i2).astype(F32)
    e_hi = jnp.where(first_lo, i2, i1).astype(F32)
    w_lo = jnp.where(first_lo, w1, w2)
    w_hi = jnp.where(first_lo, w2, w1)
    route_ref[...] = jnp.where(lane == 0, e_lo, jnp.where(lane == 1, e_hi,
                               jnp.where(lane == 2, w_lo, jnp.where(lane == 3, w_hi, 0.0))))


def _oproj_call(a, m, x, ogm, wo, g, b, wr=None):
    n = x.shape[0]
    in_specs = [_row_spec(TM, D_SG), _row_spec(TM, D_ATT), _row_spec(TM, D_MODEL),
                _const_spec(ogm.shape), _const_spec(wo.shape), _const_spec(g.shape),
                _const_spec(b.shape)]
    if wr is None:
        return pl.pallas_call(
            _oproj_body, grid=(n // TM,), in_specs=in_specs,
            out_specs=_row_spec(TM, D_MODEL),
            out_shape=jax.ShapeDtypeStruct((n, D_MODEL), F32),
            compiler_params=_params("parallel"), name="oproj",
        )(a, m, x, ogm, wo, g, b)
    return pl.pallas_call(
        _oproj_route_body, grid=(n // TM,), in_specs=in_specs + [_const_spec(wr.shape)],
        out_specs=[_row_spec(TM, D_MODEL), _row_spec(TM * ROW_TILES, LANES),
                   _row_spec(TM, LANES)],
        out_shape=[jax.ShapeDtypeStruct((n, D_MODEL), F32),
                   jax.ShapeDtypeStruct((n * ROW_TILES, LANES), F32),
                   jax.ShapeDtypeStruct((n, LANES), F32)],
        compiler_params=_params("parallel"), name="oproj_route",
    )(a, m, x, ogm, wo, g, b, wr)


def _ffn_body(x_ref, w1_ref, w3_ref, w2_ref, f_ref):
    xb = x_ref[...].astype(BF16)
    acc = jnp.zeros(f_ref.shape, F32)
    for c in range(D_FF // TF_DENSE):
        sl = slice(c * TF_DENSE, (c + 1) * TF_DENSE)
        g = jnp.dot(xb, w1_ref[:, sl], preferred_element_type=F32)
        u = jnp.dot(xb, w3_ref[:, sl], preferred_element_type=F32)
        acc = acc + jnp.dot((_silu(g) * u).astype(BF16), w2_ref[sl, :],
                            preferred_element_type=F32)
    f_ref[...] = acc


def _ffn_call(x1, w1, w3, w2):
    n = x1.shape[0]
    return pl.pallas_call(
        _ffn_body, grid=(n // TM,),
        in_specs=[_row_spec(TM, D_MODEL), _const_spec(w1.shape), _const_spec(w3.shape),
                  _const_spec(w2.shape)],
        out_specs=_row_spec(TM, D_MODEL),
        out_shape=jax.ShapeDtypeStruct((n, D_MODEL), F32),
        compiler_params=_params("parallel"), name="ffn_dense",
    )(x1, w1, w3, w2)


def _dispatch_body(dest_ref, x_ref, init_ref, xs_ref, sem):
    del init_ref
    tm = x_ref.shape[0]

    def issue(t, carry):
        for slot in range(2):
            pltpu.make_async_copy(x_ref.at[t], xs_ref.at[dest_ref[0, 0, 2 * t + slot]],
                                  sem).start()
        return carry

    lax.fori_loop(0, tm, issue, 0)
    for slot in range(2):
        pltpu.make_async_copy(x_ref, xs_ref.at[pl.ds(0, tm)], sem).wait()


def _dispatch_call(dest, x1t, xs_init):
    n = x1t.shape[0]
    return pl.pallas_call(
        _dispatch_body, grid=(n // TM,),
        in_specs=[pl.BlockSpec((1, 1, 2 * TM), lambda i: (i, 0, 0), memory_space=pltpu.SMEM),
                  pl.BlockSpec((TM, ROW_TILES, LANES), lambda i: (i, 0, 0)),
                  pl.BlockSpec(memory_space=pl.ANY)],
        out_specs=pl.BlockSpec(memory_space=pl.ANY),
        out_shape=jax.ShapeDtypeStruct(xs_init.shape, F32),
        scratch_shapes=[pltpu.SemaphoreType.DMA],
        input_output_aliases={2: 0},
        compiler_params=_params("arbitrary"), name="moe_dispatch",
    )(dest, x1t, xs_init)


def _moe_body(te_ref, tv_ref, xs_ref, w1_ref, w3_ref, w2_ref, ys_ref, xb_sc, acc_sc):
    t = pl.program_id(0)
    j = pl.program_id(1)
    tm = xb_sc.shape[0]

    @pl.when(tv_ref[t] == 1)
    def _():
        @pl.when(j == 0)
        def _():
            xb_sc[...] = _load_row_tiles(xs_ref, tm).astype(BF16)
            acc_sc[...] = jnp.zeros(acc_sc.shape, F32)

        xb = xb_sc[...]
        g = jnp.dot(xb, w1_ref[0], preferred_element_type=F32)
        u = jnp.dot(xb, w3_ref[0], preferred_element_type=F32)
        acc_sc[...] += jnp.dot((_silu(g) * u).astype(BF16), w2_ref[0],
                               preferred_element_type=F32)

    @pl.when(j == pl.num_programs(1) - 1)
    def _():
        @pl.when(tv_ref[t] == 1)
        def _():
            _store_row_tiles(ys_ref, acc_sc[...])

        @pl.when(tv_ref[t] == 0)
        def _():
            ys_ref[...] = jnp.zeros(ys_ref.shape, F32)


def _moe_call(tile_expert, tile_valid, xs, w1, w3, w2):
    n_tiles = xs.shape[0] // (TM_MOE * ROW_TILES)
    n_ff = D_FF_EXPERT // TF_MOE

    def ff_idx(t, j, te, tv):
        return j * tv[t] + (n_ff - 1) * (1 - tv[t])

    return pl.pallas_call(
        _moe_body,
        grid_spec=pltpu.PrefetchScalarGridSpec(
            num_scalar_prefetch=2,
            grid=(n_tiles, n_ff),
            in_specs=[
                pl.BlockSpec((TM_MOE * ROW_TILES, LANES), lambda t, j, te, tv: (t, 0)),
                pl.BlockSpec((1, D_MODEL, TF_MOE),
                             lambda t, j, te, tv: (te[t], 0, ff_idx(t, j, te, tv))),
                pl.BlockSpec((1, D_MODEL, TF_MOE),
                             lambda t, j, te, tv: (te[t], 0, ff_idx(t, j, te, tv))),
                pl.BlockSpec((1, TF_MOE, D_MODEL),
                             lambda t, j, te, tv: (te[t], ff_idx(t, j, te, tv), 0)),
            ],
            out_specs=pl.BlockSpec((TM_MOE * ROW_TILES, LANES), lambda t, j, te, tv: (t, 0)),
            scratch_shapes=[pltpu.VMEM((TM_MOE, D_MODEL), BF16),
                            pltpu.VMEM((TM_MOE, D_MODEL), F32)],
        ),
        out_shape=jax.ShapeDtypeStruct(xs.shape, F32),
        compiler_params=_params("arbitrary", "arbitrary"), name="moe_experts",
    )(tile_expert, tile_valid, xs, w1, w3, w2)


def _combine_body(dest_ref, route_ref, ys_ref, f_ref, buf, sem):
    tm = f_ref.shape[0]

    def issue(t, carry):
        for slot in range(2):
            pltpu.make_async_copy(ys_ref.at[dest_ref[0, 0, 2 * t + slot]], buf.at[slot, t],
                                  sem).start()
        return carry

    lax.fori_loop(0, tm, issue, 0)
    for slot in range(2):
        pltpu.make_async_copy(ys_ref.at[pl.ds(0, tm)], buf.at[slot], sem).wait()
    route = route_ref[...]
    w_lo = route[:, 2:3]
    w_hi = route[:, 3:4]
    f_ref[...] = jnp.concatenate(
        [w_lo * buf[0, :, j, :] + w_hi * buf[1, :, j, :] for j in range(ROW_TILES)], axis=1)


def _combine_call(dest, route, ys):
    n = route.shape[0]
    return pl.pallas_call(
        _combine_body, grid=(n // TM,),
        in_specs=[pl.BlockSpec((1, 1, 2 * TM), lambda i: (i, 0, 0), memory_space=pltpu.SMEM),
                  _row_spec(TM, LANES),
                  pl.BlockSpec(memory_space=pl.ANY)],
        out_specs=_row_spec(TM, D_MODEL),
        out_shape=jax.ShapeDtypeStruct((n, D_MODEL), F32),
        scratch_shapes=[pltpu.VMEM((2, TM, ROW_TILES, LANES), F32), pltpu.SemaphoreType.DMA],
        compiler_params=_params("arbitrary"), name="moe_combine",
    )(dest, route, ys)


def _route_plan(route):
    n = route.shape[0]
    e_pair = route[:, :2].astype(jnp.int32)
    onehot = (e_pair[:, :, None] == jnp.arange(N_EXPERTS)[None, None, :]).any(axis=1)
    onehot = onehot.astype(jnp.int32)
    rank = jnp.cumsum(onehot, axis=0) - onehot
    counts = jnp.sum(onehot, axis=0)
    tiles = (counts + TM_MOE - 1) // TM_MOE
    tile_end = jnp.cumsum(tiles)
    row0 = (tile_end - tiles) * TM_MOE
    dest = jnp.take_along_axis(row0[None, :] + rank, e_pair, axis=1)
    n_tiles = 2 * n // TM_MOE + N_EXPERTS
    t_idx = jnp.arange(n_tiles)
    tile_expert = jnp.minimum(jnp.sum(t_idx[:, None] >= tile_end[None, :], axis=1),
                              N_EXPERTS - 1).astype(jnp.int32)
    tile_valid = (t_idx < tile_end[-1]).astype(jnp.int32)
    last_expert = jnp.max(jnp.where(tiles > 0, jnp.arange(N_EXPERTS), 0)).astype(jnp.int32)
    tile_expert = jnp.where(tile_valid == 1, tile_expert, last_expert)
    return dest.astype(jnp.int32).reshape(n // TM, 1, 2 * TM), tile_expert, tile_valid, n_tiles


def _ple_body(x1_ref, f_ref, p_ref, g_ref, b_ref, wg_ref, bg_ref, wp_ref, o_ref):
    x2 = _layer_norm(DN_ALPHA * x1_ref[...] + f_ref[...], g_ref[...], b_ref[...])
    gate = jax.nn.sigmoid(
        jnp.dot(x2.astype(BF16), wg_ref[...], preferred_element_type=F32) + bg_ref[...])
    emb = jnp.dot(p_ref[...].astype(BF16), wp_ref[...], preferred_element_type=F32)
    o_ref[...] = x2 + gate * emb


def _ple_call(x1, f, p, g, b, wg, bg, wp):
    n = x1.shape[0]
    return pl.pallas_call(
        _ple_body, grid=(n // TM,),
        in_specs=[_row_spec(TM, D_MODEL), _row_spec(TM, D_MODEL), _row_spec(TM, D_PLE),
                  _const_spec(g.shape), _const_spec(b.shape), _const_spec(wg.shape),
                  _const_spec(bg.shape), _const_spec(wp.shape)],
        out_specs=_row_spec(TM, D_MODEL),
        out_shape=jax.ShapeDtypeStruct((n, D_MODEL), F32),
        compiler_params=_params("parallel"), name="ln_ple",
    )(x1, f, p, g, b, wg, bg, wp)


def _rope_tables(positions):
    inv_freq = 1.0 / (ROPE_THETA ** (jnp.arange(0, QK_ROPE, 2, dtype=F32) / QK_ROPE))
    ang = positions.astype(F32).reshape(-1, 1) * inv_freq
    cos, sin = jnp.cos(ang), jnp.sin(ang)
    n = cos.shape[0]
    ones = jnp.ones((n, ROPE_LANE0), F32)
    zeros = jnp.zeros((n, ROPE_LANE0), F32)
    tail1 = jnp.ones((n, LANES - ROPE_LANE0 - QK_ROPE), F32)
    tail0 = jnp.zeros((n, LANES - ROPE_LANE0 - QK_ROPE), F32)
    rc = jnp.concatenate([ones, cos, cos, tail1], axis=1)
    rs = jnp.concatenate([zeros, -sin, sin, tail0], axis=1)
    return rc, rs


def _row(v):
    return v.reshape(1, -1).astype(F32)


def kernel(x, p, positions, w_in, sg_v_g, sg_v_b, sg_w_s, sg_b_s, q_norm_g, kv_norm_g, w_uq, w_ukv, out_g, w_o, ln1_g, ln1_b, ffn_w1, ffn_w3, ffn_w2, moe_w_r, moe_w1, moe_w3, moe_w2, ln2_g, ln2_b, ple_w_g, ple_b_g, ple_w_p):
    bsz, seq, _ = x.shape
    n = bsz * seq
    xf = x.reshape(n, D_MODEL)
    rc, rs = _rope_tables(positions)

    for i in range(DEPTH):
        win = jnp.concatenate(
            [w_in[i, :, :C_KR0], jnp.zeros((D_MODEL, ROPE_LANE0), F32), w_in[i, :, C_KR0:],
             jnp.zeros((D_MODEL, LANES - ROPE_LANE0 - QK_ROPE), F32)], axis=1).astype(BF16)
        wuq = jnp.pad(w_uq[i].reshape(Q_LORA, N_HEADS, QK_NOPE + QK_ROPE),
                      ((0, 0), (0, 0), (0, HEAD_PAD - QK_NOPE - QK_ROPE)))
        wuq = wuq.reshape(Q_LORA, D_QK_PAD).astype(BF16)
        wukv = w_ukv[i].reshape(KV_LORA, N_HEADS, QK_NOPE + V_DIM)
        wuk = jnp.pad(wukv[:, :, :QK_NOPE], ((0, 0), (0, 0), (0, HEAD_PAD - QK_NOPE)))
        wuk = wuk.reshape(KV_LORA, D_QK_PAD).astype(BF16)
        wuv = jnp.pad(wukv[:, :, QK_NOPE:], ((0, 0), (0, 0), (0, HEAD_PAD - V_DIM)))
        wuv = wuv.reshape(KV_LORA, D_QK_PAD).astype(BF16)
        bexp = jnp.repeat(sg_b_s[i].T, SG_DIM, axis=1)
        a, q, k, v = _proj_call(xf, win, _row(sg_v_g[i]), _row(sg_v_b[i]), sg_w_s[i], bexp,
                                _row(q_norm_g[i]), _row(kv_norm_g[i]), wuq, wuk, wuv, rc, rs,
                                _row(out_g[i, :D_SG]))
        m = _attn_call(q.reshape(bsz, seq, D_QK_PAD), k.reshape(bsz, seq, D_QK_PAD),
                       v.reshape(bsz, seq, D_QK_PAD)).reshape(n, D_ATT)

        j = i // 2
        mix_args = (a, m, xf, _row(out_g[i, D_SG:]), w_o[i].astype(BF16), _row(ln1_g[i]),
                    _row(ln1_b[i]))
        if i % 2 == 0:
            x1 = _oproj_call(*mix_args)
            f = _ffn_call(x1, ffn_w1[j].astype(BF16), ffn_w3[j].astype(BF16),
                          ffn_w2[j].astype(BF16))
        else:
            wr = jnp.pad(moe_w_r[j], ((0, 0), (0, LANES - N_EXPERTS)))
            x1, x1t, route = _oproj_call(*mix_args, wr=wr)
            dest, tile_expert, tile_valid, n_tiles = _route_plan(route)
            xs0 = jnp.zeros((n_tiles * TM_MOE, ROW_TILES, LANES), F32)
            xs = _dispatch_call(dest, x1t.reshape(n, ROW_TILES, LANES), xs0)
            ys = _moe_call(tile_expert, tile_valid, xs.reshape(-1, LANES),
                           moe_w1[j].astype(BF16), moe_w3[j].astype(BF16),
                           moe_w2[j].astype(BF16))
            f = _combine_call(dest, route, ys.reshape(-1, ROW_TILES, LANES))

        xf = _ple_call(x1, f, p[i].reshape(n, D_PLE), _row(ln2_g[i]), _row(ln2_b[i]),
                       ple_w_g[i].astype(BF16), _row(ple_b_g[i]), ple_w_p[i].astype(BF16))

    return xf.reshape(bsz, seq, D_MODEL)
```

```python
import functools
import math

import jax
import jax.numpy as jnp
from jax import lax
from jax.experimental import pallas as pl
from jax.experimental.pallas import tpu as pltpu

F32 = jnp.float32
BF16 = jnp.bfloat16

D_MODEL = 1024
DEPTH = 4
D_PLE = 256
N_SG = 8
SG_DIM = 64
D_SG = N_SG * SG_DIM
CHUNK = 128
N_HEADS = 8
QK_NOPE = 64
QK_ROPE = 32
V_DIM = 64
Q_LORA = 256
KV_LORA = 128
D_ATT = N_HEADS * V_DIM
ROPE_THETA = 10000.0
D_FF = 2816
N_EXPERTS = 8
D_FF_EXPERT = 3584
DN_ALPHA = (2.0 * DEPTH) ** 0.25
EPS = 1e-6
ATT_SCALE = (QK_NOPE + QK_ROPE) ** -0.5 * math.log2(math.e)

LANES = 128
SUBLANES = 8
HEAD_PAD = LANES
D_QK_PAD = N_HEADS * HEAD_PAD
C_Q0 = 2 * D_SG
C_KV0 = C_Q0 + Q_LORA
C_KR0 = C_KV0 + KV_LORA
D_IN_PAD = C_KR0 + LANES
ROPE_LANE0 = QK_NOPE
ROPE_HALF = QK_ROPE // 2

TM = 512
TQ = 512
TF_DENSE = 256
TF_MOE = 256
TM_MOE = 512
VMEM_LIMIT = 56 * 1024 * 1024


def _params(*sem):
    return pltpu.CompilerParams(dimension_semantics=sem, vmem_limit_bytes=VMEM_LIMIT)


def _const_spec(shape):
    nd = len(shape)
    return pl.BlockSpec(shape, lambda *_: (0,) * nd, pipeline_mode=pl.Buffered(1))


def _row_spec(tm, width):
    return pl.BlockSpec((tm, width), lambda i: (i, 0))


def _layer_norm(x, g, b):
    mu = jnp.mean(x, axis=-1, keepdims=True)
    xc = x - mu
    var = jnp.mean(xc * xc, axis=-1, keepdims=True)
    return xc * lax.rsqrt(var + EPS) * g + b


def _rms_norm(x, g):
    return x * lax.rsqrt(jnp.mean(x * x, axis=-1, keepdims=True) + EPS) * g


def _gelu(x):
    return 0.5 * x * (1.0 + lax.erf(x * (1.0 / math.sqrt(2.0))))


def _silu(x):
    return x * jax.nn.sigmoid(x)


def _rope_tile(blk, rc, rs, up):
    nxt = pltpu.roll(blk, LANES - ROPE_HALF, 1)
    prv = pltpu.roll(blk, ROPE_HALF, 1)
    return blk * rc + jnp.where(up, nxt, prv) * rs


def _proj_body(x_ref, win_ref, vg_ref, vb_ref, ws_ref, bexp_ref, qg_ref, kvg_ref,
               wuq_ref, wuk_ref, wuv_ref, rc_ref, rs_ref, og_ref,
               a_ref, q_ref, k_ref, v_ref):
    tm = x_ref.shape[0]
    h = jnp.dot(x_ref[...].astype(BF16), win_ref[...], preferred_element_type=F32)

    uv = _gelu(h[:, :C_Q0])
    u = uv[:, :D_SG]
    v = _layer_norm(uv[:, D_SG:], vg_ref[...], vb_ref[...]).astype(BF16)
    t_idx = lax.broadcasted_iota(jnp.int32, (CHUNK, CHUNK), 0)
    s_idx = lax.broadcasted_iota(jnp.int32, (CHUNK, CHUNK), 1)
    causal = s_idx <= t_idx
    first_group = s_idx < SG_DIM
    wpair = [
        jnp.concatenate([jnp.where(causal, ws_ref[2 * j], 0.0),
                         jnp.where(causal, ws_ref[2 * j + 1], 0.0)], axis=0).astype(BF16)
        for j in range(N_SG // 2)
    ]
    rows = []
    for c in range(tm // CHUNK):
        blks = []
        for j in range(N_SG // 2):
            vblk = v[c * CHUNK:(c + 1) * CHUNK, j * LANES:(j + 1) * LANES]
            mm = jnp.dot(wpair[j], vblk, preferred_element_type=F32)
            blks.append(jnp.where(first_group, mm[:CHUNK], mm[CHUNK:]))
        rows.append(jnp.concatenate(blks, axis=1) + bexp_ref[...])
    mixed = jnp.concatenate(rows, axis=0)
    a_ref[...] = _rms_norm(u * mixed, og_ref[...]).astype(BF16)

    rc = rc_ref[...]
    rs = rs_ref[...]
    up = lax.broadcasted_iota(jnp.int32, (tm, LANES), 1) < ROPE_LANE0 + ROPE_HALF

    cq = _rms_norm(h[:, C_Q0:C_KV0], qg_ref[...]).astype(BF16)
    q = jnp.dot(cq, wuq_ref[...], preferred_element_type=F32)
    for hd in range(N_HEADS):
        sl = slice(hd * HEAD_PAD, (hd + 1) * HEAD_PAD)
        q_ref[:, sl] = (_rope_tile(q[:, sl], rc, rs, up) * ATT_SCALE).astype(BF16)

    ckv = _rms_norm(h[:, C_KV0:C_KR0], kvg_ref[...]).astype(BF16)
    kn = jnp.dot(ckv, wuk_ref[...], preferred_element_type=F32)
    kr = _rope_tile(h[:, C_KR0:], rc, rs, up)
    for hd in range(N_HEADS):
        sl = slice(hd * HEAD_PAD, (hd + 1) * HEAD_PAD)
        k_ref[:, sl] = (kn[:, sl] + kr).astype(BF16)
    vv = jnp.dot(ckv, wuv_ref[...], preferred_element_type=F32)
    value_lanes = lax.broadcasted_iota(jnp.int32, (tm, LANES), 1) < V_DIM
    for hd in range(N_HEADS):
        sl = slice(hd * HEAD_PAD, (hd + 1) * HEAD_PAD)
        v_ref[:, sl] = jnp.where(value_lanes, vv[:, sl], 1.0).astype(BF16)


def _proj_call(x, win, vg, vb, ws, bexp, qg, kvg, wuq, wuk, wuv, rc, rs, og):
    n = x.shape[0]
    return pl.pallas_call(
        _proj_body,
        grid=(n // TM,),
        in_specs=[
            _row_spec(TM, D_MODEL),
            _const_spec(win.shape), _const_spec(vg.shape), _const_spec(vb.shape),
            _const_spec(ws.shape), _const_spec(bexp.shape), _const_spec(qg.shape),
            _const_spec(kvg.shape), _const_spec(wuq.shape), _const_spec(wuk.shape),
            _const_spec(wuv.shape),
            _row_spec(TM, LANES), _row_spec(TM, LANES),
            _const_spec(og.shape),
        ],
        out_specs=[_row_spec(TM, D_SG)] + [_row_spec(TM, D_QK_PAD)] * 3,
        out_shape=[jax.ShapeDtypeStruct((n, D_SG), BF16)]
        + [jax.ShapeDtypeStruct((n, D_QK_PAD), BF16)] * 3,
        compiler_params=_params("parallel"),
        name="proj",
    )(x, win, vg, vb, ws, bexp, qg, kvg, wuq, wuk, wuv, rc, rs, og)


def _attn_body(q_ref, k_ref, v_ref, o_ref, m_sc, acc_sc):
    tq = q_ref.shape[1]
    qi = pl.program_id(1)

    def step(off, mask):
        def scores(hh):
            sl = slice(hh * HEAD_PAD, (hh + 1) * HEAD_PAD)
            return lax.dot_general(q_ref[0, :, sl], k_ref[0, pl.ds(off, tq), sl],
                                   (((1,), (1,)), ((), ())), preferred_element_type=F32)

        ahead = 2
        s_queue = [scores(hh) for hh in range(ahead)]
        for hh in range(N_HEADS):
            sl = slice(hh * HEAD_PAD, (hh + 1) * HEAD_PAD)
            s = s_queue.pop(0)
            if hh + ahead < N_HEADS:
                s_queue.append(scores(hh + ahead))
            vblk = v_ref[0, pl.ds(off, tq), sl]
            if mask is not None:
                s = jnp.where(mask, s, -jnp.inf)
                m_next = jnp.broadcast_to(jnp.max(s, axis=1, keepdims=True), (tq, LANES))
                p = jnp.exp2(s - jnp.concatenate([m_next] * (tq // LANES), axis=1)).astype(BF16)
                acc_sc[hh] = jnp.dot(p, vblk, preferred_element_type=F32)
            else:
                m_prev = m_sc[hh]
                m_next = jnp.maximum(m_prev, jnp.max(s, axis=1, keepdims=True))
                p = jnp.exp2(s - jnp.concatenate([m_next] * (tq // LANES), axis=1)).astype(BF16)
                acc_sc[hh] = jnp.exp2(m_prev - m_next) * acc_sc[hh] + jnp.dot(
                    p, vblk, preferred_element_type=F32)
            m_sc[hh] = m_next

    row = lax.broadcasted_iota(jnp.int32, (tq, tq), 0)
    col = lax.broadcasted_iota(jnp.int32, (tq, tq), 1)
    step(pl.multiple_of(qi * tq, tq), col <= row)

    def full_block(ki, carry):
        step(pl.multiple_of(ki * tq, tq), None)
        return carry

    lax.fori_loop(0, qi, full_block, 0)

    value_lanes = lax.broadcasted_iota(jnp.int32, (tq, LANES), 1) < V_DIM
    outs = []
    for hh in range(N_HEADS):
        acc = acc_sc[hh]
        row_sum = jnp.where(value_lanes, pltpu.roll(acc, V_DIM, 1), 1.0)
        outs.append(acc / row_sum)
    for pair in range(N_HEADS // 2):
        o_ref[0, :, pair * LANES:(pair + 1) * LANES] = jnp.where(
            value_lanes, outs[2 * pair], pltpu.roll(outs[2 * pair + 1], V_DIM, 1))


def _attn_call(q, k, v):
    b, s, _ = q.shape
    kv_spec = pl.BlockSpec((1, s, D_QK_PAD), lambda bi, qi: (bi, 0, 0))
    return pl.pallas_call(
        _attn_body,
        grid=(b, s // TQ),
        in_specs=[pl.BlockSpec((1, TQ, D_QK_PAD), lambda bi, qi: (bi, qi, 0)), kv_spec, kv_spec],
        out_specs=pl.BlockSpec((1, TQ, D_ATT), lambda bi, qi: (bi, qi, 0)),
        out_shape=jax.ShapeDtypeStruct((b, s, D_ATT), F32),
        scratch_shapes=[pltpu.VMEM((N_HEADS, TQ, LANES), F32)] * 2,
        compiler_params=_params("parallel", "arbitrary"),
        name="attn",
    )(q, k, v)


def _mix_out(a_ref, m_ref, x_ref, ogm_ref, wo_ref, g_ref, b_ref):
    mn = _rms_norm(m_ref[...], ogm_ref[...]).astype(BF16)
    y = jnp.dot(a_ref[...], wo_ref[:D_SG, :], preferred_element_type=F32)
    y = y + jnp.dot(mn, wo_ref[D_SG:, :], preferred_element_type=F32)
    return _layer_norm(DN_ALPHA * x_ref[...] + y, g_ref[...], b_ref[...])


def _oproj_body(a_ref, m_ref, x_ref, ogm_ref, wo_ref, g_ref, b_ref, x1_ref):
    x1_ref[...] = _mix_out(a_ref, m_ref, x_ref, ogm_ref, wo_ref, g_ref, b_ref)


def _oproj_route_body(a_ref, m_ref, x_ref, ogm_ref, wo_ref, g_ref, b_ref, wr_ref,
                      x1_ref, route_ref, cnt_ref, cnt_sc):
    @pl.when(pl.program_id(0) == 0)
    def _():
        cnt_sc[...] = jnp.zeros(cnt_sc.shape, F32)

    x1 = _mix_out(a_ref, m_ref, x_ref, ogm_ref, wo_ref, g_ref, b_ref)
    x1_ref[...] = x1
    tm = x1.shape[0]
    lane = lax.broadcasted_iota(jnp.int32, (tm, LANES), 1)
    logits = jnp.dot(x1, wr_ref[...], preferred_element_type=F32,
                     precision=lax.Precision.HIGHEST)
    lg = jnp.where(lane < N_EXPERTS, logits, -jnp.inf)
    m1 = jnp.max(lg, axis=1, keepdims=True)
    i1 = jnp.min(jnp.where(lg == m1, lane, LANES), axis=1, keepdims=True)
    lg2 = jnp.where(lane == i1, -jnp.inf, lg)
    m2 = jnp.max(lg2, axis=1, keepdims=True)
    i2 = jnp.min(jnp.where(lg2 == m2, lane, LANES), axis=1, keepdims=True)
    e = jnp.exp(m2 - m1)
    w1 = 1.0 / (1.0 + e)
    w2 = e / (1.0 + e)
    first_lo = i1 < i2
    e_lo = jnp.where(first_lo, i1, i2).astype(F32)
    e_hi = jnp.where(first_lo, i2, i1).astype(F32)
    w_lo = jnp.where(first_lo, w1, w2)
    w_hi = jnp.where(first_lo, w2, w1)

    is_lo = lane == jnp.where(first_lo, i1, i2)
    is_hi = lane == jnp.where(first_lo, i2, i1)
    onehot = jnp.where(is_lo | is_hi, 1.0, 0.0).astype(BF16)
    r_idx = lax.broadcasted_iota(jnp.int32, (tm, tm), 0)
    c_idx = lax.broadcasted_iota(jnp.int32, (tm, tm), 1)
    earlier = jnp.where(c_idx < r_idx, 1.0, 0.0).astype(BF16)
    rank = jnp.dot(earlier, onehot, preferred_element_type=F32) + cnt_sc[0:1, :]
    rank_lo = jnp.sum(jnp.where(is_lo, rank, 0.0), axis=1, keepdims=True)
    rank_hi = jnp.sum(jnp.where(is_hi, rank, 0.0), axis=1, keepdims=True)
    cnt_sc[...] += jnp.dot(jnp.ones((SUBLANES, tm), BF16), onehot, preferred_element_type=F32)
    cnt_ref[...] = cnt_sc[...]
    cols = (e_lo, e_hi, w_lo, w_hi, rank_lo, rank_hi)
    route = jnp.zeros((tm, LANES), F32)
    for idx, colv in enumerate(cols):
        route = jnp.where(lane == idx, colv, route)
    route_ref[...] = route


def _oproj_call(a, m, x, ogm, wo, g, b, wr=None):
    n = x.shape[0]
    in_specs = [_row_spec(TM, D_SG), _row_spec(TM, D_ATT), _row_spec(TM, D_MODEL),
                _const_spec(ogm.shape), _const_spec(wo.shape), _const_spec(g.shape),
                _const_spec(b.shape)]
    if wr is None:
        return pl.pallas_call(
            _oproj_body, grid=(n // TM,), in_specs=in_specs,
            out_specs=_row_spec(TM, D_MODEL),
            out_shape=jax.ShapeDtypeStruct((n, D_MODEL), F32),
            compiler_params=_params("parallel"), name="oproj",
        )(a, m, x, ogm, wo, g, b)
    return pl.pallas_call(
        _oproj_route_body, grid=(n // TM,), in_specs=in_specs + [_const_spec(wr.shape)],
        out_specs=[_row_spec(TM, D_MODEL), _row_spec(TM, LANES),
                   pl.BlockSpec((SUBLANES, LANES), lambda i: (0, 0))],
        out_shape=[jax.ShapeDtypeStruct((n, D_MODEL), F32),
                   jax.ShapeDtypeStruct((n, LANES), F32),
                   jax.ShapeDtypeStruct((SUBLANES, LANES), F32)],
        scratch_shapes=[pltpu.VMEM((SUBLANES, LANES), F32)],
        compiler_params=_params("arbitrary"), name="oproj_route",
    )(a, m, x, ogm, wo, g, b, wr)


def _ffn_body(x_ref, w1_ref, w3_ref, w2_ref, f_ref):
    xb = x_ref[...].astype(BF16)
    acc = jnp.zeros(f_ref.shape, F32)
    for c in range(D_FF // TF_DENSE):
        sl = slice(c * TF_DENSE, (c + 1) * TF_DENSE)
        g = jnp.dot(xb, w1_ref[:, sl], preferred_element_type=F32)
        u = jnp.dot(xb, w3_ref[:, sl], preferred_element_type=F32)
        acc = acc + jnp.dot((_silu(g) * u).astype(BF16), w2_ref[sl, :],
                            preferred_element_type=F32)
    f_ref[...] = acc


def _ffn_call(x1, w1, w3, w2):
    n = x1.shape[0]
    return pl.pallas_call(
        _ffn_body, grid=(n // TM,),
        in_specs=[_row_spec(TM, D_MODEL), _const_spec(w1.shape), _const_spec(w3.shape),
                  _const_spec(w2.shape)],
        out_specs=_row_spec(TM, D_MODEL),
        out_shape=jax.ShapeDtypeStruct((n, D_MODEL), F32),
        compiler_params=_params("parallel"), name="ffn_dense",
    )(x1, w1, w3, w2)


def _swiglu_half(xb, w1_ref, w3_ref, w2_ref, after_chunk=None):
    acc = None
    for c in range(w2_ref.shape[1] // TF_MOE):
        sl = slice(c * TF_MOE, (c + 1) * TF_MOE)
        g = jnp.dot(xb, w1_ref[0, :, sl], preferred_element_type=F32)
        u = jnp.dot(xb, w3_ref[0, :, sl], preferred_element_type=F32)
        part = jnp.dot((_silu(g) * u).astype(BF16), w2_ref[0, sl, :], preferred_element_type=F32)
        acc = part if acc is None else acc + part
        if after_chunk is not None:
            after_chunk(c, part)
    return acc


def _moe_body(te_ref, tv_ref, src0_ref, src1_ref, dst_ref, x_hbm, w1_ref, w3_ref, w2_ref, y_hbm,
              xbuf, xb_sc, acc_sc, ybuf, gsem, ssem):
    del te_ref
    t = pl.program_id(0)
    h = pl.program_id(1)
    tm = xb_sc.shape[0]
    last_t = pl.num_programs(0) - 1
    n_chunks = w2_ref.shape[1] // TF_MOE
    rows_per_chunk = -(-tm // (n_chunks - 1))

    def start_gather(src_ref, lo=0, hi=tm):
        for r in range(lo, hi):
            pltpu.make_async_copy(x_hbm.at[pl.ds(src_ref[0, 0, r], 1), :],
                                  xbuf.at[pl.ds(r, 1), :], gsem).start()

    def start_scatter(lo=0, hi=tm):
        for r in range(lo, hi):
            pltpu.make_async_copy(ybuf.at[pl.ds(r, 1), :],
                                  y_hbm.at[pl.ds(dst_ref[0, 0, r], 1), :], ssem).start()

    def wait_gather():
        pltpu.make_async_copy(x_hbm.at[pl.ds(0, tm), :], xbuf.at[pl.ds(0, tm), :], gsem).wait()

    def wait_scatter():
        pltpu.make_async_copy(ybuf.at[pl.ds(0, tm), :], y_hbm.at[pl.ds(0, tm), :], ssem).wait()

    @pl.when((t == 0) & (h == 0))
    def _():
        ybuf[...] = jnp.zeros(ybuf.shape, F32)
        start_gather(src0_ref)

    @pl.when(h == 0)
    def _():
        wait_gather()
        xb_sc[...] = xbuf[0:tm, :].astype(BF16)

    valid = tv_ref[t] == 1

    @pl.when((h == 0) & valid)
    def _():
        def row_copies_after(c, part):
            pin = part[0:SUBLANES, 0:LANES]
            xbuf[tm:tm + SUBLANES, 0:LANES] = pin
            ybuf[tm:tm + SUBLANES, 0:LANES] = pin
            lo, hi = min(c * rows_per_chunk, tm), min((c + 1) * rows_per_chunk, tm)
            start_gather(src1_ref, lo, hi)
            start_scatter(lo, hi)

        acc_sc[...] = _swiglu_half(xb_sc[...], w1_ref, w3_ref, w2_ref, row_copies_after)

    @pl.when((h == 0) & jnp.logical_not(valid))
    def _():
        start_gather(src1_ref)
        start_scatter()

    @pl.when((h == 1) & valid)
    def _():
        y = acc_sc[...] + _swiglu_half(xb_sc[...], w1_ref, w3_ref, w2_ref)
        wait_scatter()
        ybuf[0:tm, :] = y

    @pl.when((h == 1) & jnp.logical_not(valid))
    def _():
        wait_scatter()

    @pl.when((h == 1) & (t == last_t))
    def _():
        wait_gather()


def _moe_call(tile_expert, tile_valid, src_rows, dst_rows, x1, w1, w3, w2):
    n = x1.shape[0]
    n_steps = dst_rows.shape[0]
    f_half = D_FF_EXPERT // 2

    def half_idx(t, h, tv):
        return h * tv[t] + (1 - tv[t])

    def row_block(offset):
        return pl.BlockSpec((1, 1, TM_MOE), lambda t, h, te, tv: (t + offset, 0, 0),
                            memory_space=pltpu.SMEM)

    return pl.pallas_call(
        _moe_body,
        grid_spec=pltpu.PrefetchScalarGridSpec(
            num_scalar_prefetch=2,
            grid=(n_steps, 2),
            in_specs=[
                row_block(0), row_block(1), row_block(0),
                pl.BlockSpec(memory_space=pl.ANY),
                pl.BlockSpec((1, D_MODEL, f_half),
                             lambda t, h, te, tv: (te[t], 0, half_idx(t, h, tv))),
                pl.BlockSpec((1, D_MODEL, f_half),
                             lambda t, h, te, tv: (te[t], 0, half_idx(t, h, tv))),
                pl.BlockSpec((1, f_half, D_MODEL),
                             lambda t, h, te, tv: (te[t], half_idx(t, h, tv), 0)),
            ],
            out_specs=pl.BlockSpec(memory_space=pl.ANY),
            scratch_shapes=[pltpu.VMEM((TM_MOE + SUBLANES, D_MODEL), F32),
                            pltpu.VMEM((TM_MOE, D_MODEL), BF16),
                            pltpu.VMEM((TM_MOE, D_MODEL), F32),
                            pltpu.VMEM((TM_MOE + SUBLANES, D_MODEL), F32),
                            pltpu.SemaphoreType.DMA, pltpu.SemaphoreType.DMA],
        ),
        out_shape=jax.ShapeDtypeStruct((2 * n + TM_MOE, D_MODEL), F32),
        compiler_params=_params("arbitrary", "arbitrary"), name="moe_experts",
    )(tile_expert, tile_valid, src_rows, src_rows, dst_rows, x1, w1, w3, w2)


def _route_plan(route, counts):
    n = route.shape[0]
    n_tiles = 2 * n // TM_MOE + N_EXPERTS - 1
    cnt = counts[0, :N_EXPERTS].astype(jnp.int32)
    tiles = (cnt + TM_MOE - 1) // TM_MOE
    tile_end = jnp.cumsum(tiles)
    row0 = (tile_end - tiles) * TM_MOE
    e_pair = route[:, 0:2].astype(jnp.int32)
    rank = route[:, 4:6].astype(jnp.int32)
    dest = jnp.sum(jnp.where(e_pair[:, :, None] == jnp.arange(N_EXPERTS), row0, 0), axis=2) + rank
    tok = jnp.arange(n, dtype=jnp.int32)
    spare = 2 * n + jnp.arange(n_tiles * TM_MOE, dtype=jnp.int32) % TM_MOE
    dst_rows = spare.at[dest[:, 0]].set(tok, unique_indices=True)
    dst_rows = dst_rows.at[dest[:, 1]].set(n + tok, unique_indices=True)
    src_rows = jnp.where(dst_rows < 2 * n, dst_rows % n, dst_rows - 2 * n)
    dst_rows = jnp.concatenate([spare[:TM_MOE], dst_rows])
    src_rows = jnp.concatenate([src_rows, jnp.zeros((2 * TM_MOE,), jnp.int32)])
    t_idx = jnp.arange(n_tiles + 1)
    tile_valid = (t_idx < tile_end[-1]).astype(jnp.int32)
    tile_expert = jnp.minimum(jnp.sum(t_idx[:, None] >= tile_end[None, :], axis=1), N_EXPERTS - 1)
    last_expert = jnp.max(jnp.where(tiles > 0, jnp.arange(N_EXPERTS), 0))
    tile_expert = jnp.where(tile_valid == 1, tile_expert, last_expert).astype(jnp.int32)
    return (tile_expert, tile_valid, src_rows.reshape(n_tiles + 2, 1, TM_MOE),
            dst_rows.reshape(n_tiles + 1, 1, TM_MOE))


def _ple_tail(x1, f, p_ref, g_ref, b_ref, wg_ref, bg_ref, wp_ref, o_ref):
    x2 = _layer_norm(DN_ALPHA * x1 + f, g_ref[...], b_ref[...])
    gate = jax.nn.sigmoid(
        jnp.dot(x2.astype(BF16), wg_ref[...], preferred_element_type=F32) + bg_ref[...])
    emb = jnp.dot(p_ref[...].astype(BF16), wp_ref[...], preferred_element_type=F32)
    o_ref[...] = x2 + gate * emb


def _ple_body(x1_ref, f_ref, *rest):
    _ple_tail(x1_ref[...], f_ref[...], *rest)


def _ple_moe_body(x1_ref, ylo_ref, yhi_ref, route_ref, *rest):
    route = route_ref[...]
    f = route[:, 2:3] * ylo_ref[...] + route[:, 3:4] * yhi_ref[...]
    _ple_tail(x1_ref[...], f, *rest)


def _ple_call(x1, f, p, g, b, wg, bg, wp, route=None):
    n = x1.shape[0]
    tail_specs = [_row_spec(TM, D_PLE), _const_spec(g.shape), _const_spec(b.shape),
                  _const_spec(wg.shape), _const_spec(bg.shape), _const_spec(wp.shape)]
    if route is None:
        body, lead, lead_specs = _ple_body, (x1, f), [_row_spec(TM, D_MODEL)] * 2
    else:
        hi_spec = pl.BlockSpec((TM, D_MODEL), lambda i: (i + n // TM, 0))
        body, lead = _ple_moe_body, (x1, f, f, route)
        lead_specs = [_row_spec(TM, D_MODEL), _row_spec(TM, D_MODEL), hi_spec,
                      _row_spec(TM, LANES)]
    return pl.pallas_call(
        body, grid=(n // TM,),
        in_specs=lead_specs + tail_specs,
        out_specs=_row_spec(TM, D_MODEL),
        out_shape=jax.ShapeDtypeStruct((n, D_MODEL), F32),
        compiler_params=_params("parallel"), name="ln_ple",
    )(*lead, p, g, b, wg, bg, wp)


def _rope_tables(positions):
    inv_freq = 1.0 / (ROPE_THETA ** (jnp.arange(0, QK_ROPE, 2, dtype=F32) / QK_ROPE))
    ang = positions.astype(F32).reshape(-1, 1) * inv_freq
    cos, sin = jnp.cos(ang), jnp.sin(ang)
    n = cos.shape[0]
    ones = jnp.ones((n, ROPE_LANE0), F32)
    zeros = jnp.zeros((n, ROPE_LANE0), F32)
    tail1 = jnp.ones((n, LANES - ROPE_LANE0 - QK_ROPE), F32)
    tail0 = jnp.zeros((n, LANES - ROPE_LANE0 - QK_ROPE), F32)
    rc = jnp.concatenate([ones, cos, cos, tail1], axis=1)
    rs = jnp.concatenate([zeros, -sin, sin, tail0], axis=1)
    return rc, rs


def _row(v):
    return v.reshape(1, -1).astype(F32)


def kernel(x, p, positions, w_in, sg_v_g, sg_v_b, sg_w_s, sg_b_s, q_norm_g, kv_norm_g, w_uq, w_ukv, out_g, w_o, ln1_g, ln1_b, ffn_w1, ffn_w3, ffn_w2, moe_w_r, moe_w1, moe_w3, moe_w2, ln2_g, ln2_b, ple_w_g, ple_b_g, ple_w_p):
    bsz, seq, _ = x.shape
    n = bsz * seq
    xf = x.reshape(n, D_MODEL)
    rc, rs = _rope_tables(positions)

    for i in range(DEPTH):
        win = jnp.concatenate(
            [w_in[i, :, :C_KR0], jnp.zeros((D_MODEL, ROPE_LANE0), F32), w_in[i, :, C_KR0:],
             jnp.zeros((D_MODEL, LANES - ROPE_LANE0 - QK_ROPE), F32)], axis=1).astype(BF16)
        wuq = jnp.pad(w_uq[i].reshape(Q_LORA, N_HEADS, QK_NOPE + QK_ROPE),
                      ((0, 0), (0, 0), (0, HEAD_PAD - QK_NOPE - QK_ROPE)))
        wuq = wuq.reshape(Q_LORA, D_QK_PAD).astype(BF16)
        wukv = w_ukv[i].reshape(KV_LORA, N_HEADS, QK_NOPE + V_DIM)
        wuk = jnp.pad(wukv[:, :, :QK_NOPE], ((0, 0), (0, 0), (0, HEAD_PAD - QK_NOPE)))
        wuk = wuk.reshape(KV_LORA, D_QK_PAD).astype(BF16)
        wuv = jnp.pad(wukv[:, :, QK_NOPE:], ((0, 0), (0, 0), (0, HEAD_PAD - V_DIM)))
        wuv = wuv.reshape(KV_LORA, D_QK_PAD).astype(BF16)
        bexp = jnp.repeat(sg_b_s[i].T, SG_DIM, axis=1)
        a, q, k, v = _proj_call(xf, win, _row(sg_v_g[i]), _row(sg_v_b[i]), sg_w_s[i], bexp,
                                _row(q_norm_g[i]), _row(kv_norm_g[i]), wuq, wuk, wuv, rc, rs,
                                _row(out_g[i, :D_SG]))
        m = _attn_call(q.reshape(bsz, seq, D_QK_PAD), k.reshape(bsz, seq, D_QK_PAD),
                       v.reshape(bsz, seq, D_QK_PAD)).reshape(n, D_ATT)

        j = i // 2
        mix_args = (a, m, xf, _row(out_g[i, D_SG:]), w_o[i].astype(BF16), _row(ln1_g[i]),
                    _row(ln1_b[i]))
        if i % 2 == 0:
            x1 = _oproj_call(*mix_args)
            f = _ffn_call(x1, ffn_w1[j].astype(BF16), ffn_w3[j].astype(BF16),
                          ffn_w2[j].astype(BF16))
            route = None
        else:
            wr = jnp.pad(moe_w_r[j], ((0, 0), (0, LANES - N_EXPERTS)))
            x1, route, counts = _oproj_call(*mix_args, wr=wr)
            f = _moe_call(*_route_plan(route, counts), x1, moe_w1[j].astype(BF16),
                          moe_w3[j].astype(BF16), moe_w2[j].astype(BF16))

        xf = _ple_call(x1, f, p[i].reshape(n, D_PLE), _row(ln2_g[i]), _row(ln2_b[i]),
                       ple_w_g[i].astype(BF16), _row(ple_b_g[i]), ple_w_p[i].astype(BF16),
                       route=route)

    return xf.reshape(bsz, seq, D_MODEL)
```

```python
import functools
import math

import jax
import jax.numpy as jnp
from jax import lax
from jax.experimental import pallas as pl
from jax.experimental.pallas import tpu as pltpu

F32 = jnp.float32
BF16 = jnp.bfloat16

D_MODEL = 1024
DEPTH = 4
D_PLE = 256
N_SG = 8
SG_DIM = 64
D_SG = N_SG * SG_DIM
CHUNK = 128
N_HEADS = 8
QK_NOPE = 64
QK_ROPE = 32
V_DIM = 64
Q_LORA = 256
KV_LORA = 128
D_ATT = N_HEADS * V_DIM
ROPE_THETA = 10000.0
D_FF = 2816
N_EXPERTS = 8
D_FF_EXPERT = 3584
DN_ALPHA = (2.0 * DEPTH) ** 0.25
EPS = 1e-6
ATT_SCALE = (QK_NOPE + QK_ROPE) ** -0.5 * math.log2(math.e)

LANES = 128
SUBLANES = 8
HEAD_PAD = LANES
D_QK_PAD = N_HEADS * HEAD_PAD
C_Q0 = 2 * D_SG
C_KV0 = C_Q0 + Q_LORA
C_KR0 = C_KV0 + KV_LORA
D_IN_PAD = C_KR0 + LANES
ROPE_LANE0 = QK_NOPE
ROPE_HALF = QK_ROPE // 2

TM = 512
SUB = 256
TQ = 512
TF_DENSE = 256
TF_MOE = 256
TM_MOE = 512
VMEM_LIMIT = 56 * 1024 * 1024


def _params(*sem):
    return pltpu.CompilerParams(dimension_semantics=sem, vmem_limit_bytes=VMEM_LIMIT)


def _const_spec(shape):
    nd = len(shape)
    return pl.BlockSpec(shape, lambda *_: (0,) * nd, pipeline_mode=pl.Buffered(1))


def _row_spec(tm, width):
    return pl.BlockSpec((tm, width), lambda i: (i, 0))


def _layer_norm(x, g, b):
    mu = jnp.mean(x, axis=-1, keepdims=True)
    xc = x - mu
    var = jnp.mean(xc * xc, axis=-1, keepdims=True)
    return xc * lax.rsqrt(var + EPS) * g + b


def _rms_norm(x, g):
    return x * lax.rsqrt(jnp.mean(x * x, axis=-1, keepdims=True) + EPS) * g


def _gelu(x):
    return 0.5 * x * (1.0 + lax.erf(x * (1.0 / math.sqrt(2.0))))


def _silu(x):
    return x * jax.nn.sigmoid(x)


def _rope_tile(blk, rc, rs, up):
    nxt = pltpu.roll(blk, LANES - ROPE_HALF, 1)
    prv = pltpu.roll(blk, ROPE_HALF, 1)
    return blk * rc + jnp.where(up, nxt, prv) * rs


def _proj_body(x_ref, win_ref, vg_ref, vb_ref, ws_ref, bexp_ref, qg_ref, kvg_ref,
               wuq_ref, wuk_ref, wuv_ref, rc_ref, rs_ref, og_ref,
               a_ref, q_ref, k_ref, v_ref):
    tm = x_ref.shape[0]
    t_idx = lax.broadcasted_iota(jnp.int32, (CHUNK, CHUNK), 0)
    s_idx = lax.broadcasted_iota(jnp.int32, (CHUNK, CHUNK), 1)
    causal = s_idx <= t_idx
    first_group = s_idx < SG_DIM
    wpair = [
        jnp.concatenate([jnp.where(causal, ws_ref[2 * j], 0.0),
                         jnp.where(causal, ws_ref[2 * j + 1], 0.0)], axis=0).astype(BF16)
        for j in range(N_SG // 2)
    ]
    lane = lax.broadcasted_iota(jnp.int32, (SUB, LANES), 1)
    up = lane < ROPE_LANE0 + ROPE_HALF
    value_lanes = lane < V_DIM

    def in_proj(rows):
        return jnp.dot(x_ref[rows, :].astype(BF16), win_ref[...], preferred_element_type=F32)

    def rest(rows, h):
        uv = _gelu(h[:, :C_Q0])
        u = uv[:, :D_SG]
        v = _layer_norm(uv[:, D_SG:], vg_ref[...], vb_ref[...]).astype(BF16)
        mixed = []
        for c in range(SUB // CHUNK):
            blks = []
            for j in range(N_SG // 2):
                vblk = v[c * CHUNK:(c + 1) * CHUNK, j * LANES:(j + 1) * LANES]
                mm = jnp.dot(wpair[j], vblk, preferred_element_type=F32)
                blks.append(jnp.where(first_group, mm[:CHUNK], mm[CHUNK:]))
            mixed.append(jnp.concatenate(blks, axis=1) + bexp_ref[...])
        a_ref[rows, :] = _rms_norm(u * jnp.concatenate(mixed, axis=0), og_ref[...]).astype(BF16)

        rc = rc_ref[rows, :]
        rs = rs_ref[rows, :]
        cq = _rms_norm(h[:, C_Q0:C_KV0], qg_ref[...]).astype(BF16)
        q = jnp.dot(cq, wuq_ref[...], preferred_element_type=F32)
        for hd in range(N_HEADS):
            sl = slice(hd * HEAD_PAD, (hd + 1) * HEAD_PAD)
            q_ref[rows, sl] = (_rope_tile(q[:, sl], rc, rs, up) * ATT_SCALE).astype(BF16)
        ckv = _rms_norm(h[:, C_KV0:C_KR0], kvg_ref[...]).astype(BF16)
        kn = jnp.dot(ckv, wuk_ref[...], preferred_element_type=F32)
        kr = _rope_tile(h[:, C_KR0:], rc, rs, up)
        for hd in range(N_HEADS):
            sl = slice(hd * HEAD_PAD, (hd + 1) * HEAD_PAD)
            k_ref[rows, sl] = (kn[:, sl] + kr).astype(BF16)
        vv = jnp.dot(ckv, wuv_ref[...], preferred_element_type=F32)
        for hd in range(N_HEADS):
            sl = slice(hd * HEAD_PAD, (hd + 1) * HEAD_PAD)
            v_ref[rows, sl] = jnp.where(value_lanes, vv[:, sl], 1.0).astype(BF16)

    subs = [pl.ds(i * SUB, SUB) for i in range(tm // SUB)]
    h_next = in_proj(subs[0])
    for i, rows in enumerate(subs):
        h = h_next
        if i + 1 < len(subs):
            h_next = in_proj(subs[i + 1])
        rest(rows, h)


def _proj_call(x, win, vg, vb, ws, bexp, qg, kvg, wuq, wuk, wuv, rc, rs, og):
    n = x.shape[0]
    return pl.pallas_call(
        _proj_body,
        grid=(n // TM,),
        in_specs=[
            _row_spec(TM, D_MODEL),
            _const_spec(win.shape), _const_spec(vg.shape), _const_spec(vb.shape),
            _const_spec(ws.shape), _const_spec(bexp.shape), _const_spec(qg.shape),
            _const_spec(kvg.shape), _const_spec(wuq.shape), _const_spec(wuk.shape),
            _const_spec(wuv.shape),
            _row_spec(TM, LANES), _row_spec(TM, LANES),
            _const_spec(og.shape),
        ],
        out_specs=[_row_spec(TM, D_SG)] + [_row_spec(TM, D_QK_PAD)] * 3,
        out_shape=[jax.ShapeDtypeStruct((n, D_SG), BF16)]
        + [jax.ShapeDtypeStruct((n, D_QK_PAD), BF16)] * 3,
        compiler_params=_params("parallel"),
        name="proj",
    )(x, win, vg, vb, ws, bexp, qg, kvg, wuq, wuk, wuv, rc, rs, og)


def _attn_body(q_ref, k_ref, v_ref, o_ref, m_sc, acc_sc):
    tq = q_ref.shape[1]
    qi = pl.program_id(1)

    def step(off, mask):
        def scores(hh):
            sl = slice(hh * HEAD_PAD, (hh + 1) * HEAD_PAD)
            return lax.dot_general(q_ref[0, :, sl], k_ref[0, pl.ds(off, tq), sl],
                                   (((1,), (1,)), ((), ())), preferred_element_type=F32)

        ahead = 2
        s_queue = [scores(hh) for hh in range(ahead)]
        for hh in range(N_HEADS):
            sl = slice(hh * HEAD_PAD, (hh + 1) * HEAD_PAD)
            s = s_queue.pop(0)
            if hh + ahead < N_HEADS:
                s_queue.append(scores(hh + ahead))
            vblk = v_ref[0, pl.ds(off, tq), sl]
            if mask is not None:
                s = jnp.where(mask, s, -jnp.inf)
                m_next = jnp.broadcast_to(jnp.max(s, axis=1, keepdims=True), (tq, LANES))
                p = jnp.exp2(s - jnp.concatenate([m_next] * (tq // LANES), axis=1)).astype(BF16)
                acc_sc[hh] = jnp.dot(p, vblk, preferred_element_type=F32)
            else:
                m_prev = m_sc[hh]
                m_next = jnp.maximum(m_prev, jnp.max(s, axis=1, keepdims=True))
                p = jnp.exp2(s - jnp.concatenate([m_next] * (tq // LANES), axis=1)).astype(BF16)
                acc_sc[hh] = jnp.exp2(m_prev - m_next) * acc_sc[hh] + jnp.dot(
                    p, vblk, preferred_element_type=F32)
            m_sc[hh] = m_next

    row = lax.broadcasted_iota(jnp.int32, (tq, tq), 0)
    col = lax.broadcasted_iota(jnp.int32, (tq, tq), 1)
    step(pl.multiple_of(qi * tq, tq), col <= row)

    def full_block(ki, carry):
        step(pl.multiple_of(ki * tq, tq), None)
        return carry

    lax.fori_loop(0, qi, full_block, 0)

    value_lanes = lax.broadcasted_iota(jnp.int32, (tq, LANES), 1) < V_DIM
    outs = []
    for hh in range(N_HEADS):
        acc = acc_sc[hh]
        row_sum = jnp.where(value_lanes, pltpu.roll(acc, V_DIM, 1), 1.0)
        outs.append(acc / row_sum)
    for pair in range(N_HEADS // 2):
        o_ref[0, :, pair * LANES:(pair + 1) * LANES] = jnp.where(
            value_lanes, outs[2 * pair], pltpu.roll(outs[2 * pair + 1], V_DIM, 1))


def _attn_call(q, k, v):
    b, s, _ = q.shape
    kv_spec = pl.BlockSpec((1, s, D_QK_PAD), lambda bi, qi: (bi, 0, 0))
    return pl.pallas_call(
        _attn_body,
        grid=(b, s // TQ),
        in_specs=[pl.BlockSpec((1, TQ, D_QK_PAD), lambda bi, qi: (bi, qi, 0)), kv_spec, kv_spec],
        out_specs=pl.BlockSpec((1, TQ, D_ATT), lambda bi, qi: (bi, qi, 0)),
        out_shape=jax.ShapeDtypeStruct((b, s, D_ATT), F32),
        scratch_shapes=[pltpu.VMEM((N_HEADS, TQ, LANES), F32)] * 2,
        compiler_params=_params("parallel", "arbitrary"),
        name="attn",
    )(q, k, v)


def _mix_out(rows, a_ref, m_ref, x_ref, ogm_ref, wo_ref, g_ref, b_ref):
    mn = _rms_norm(m_ref[rows, :], ogm_ref[...]).astype(BF16)
    y = jnp.dot(a_ref[rows, :], wo_ref[:D_SG, :], preferred_element_type=F32)
    y = y + jnp.dot(mn, wo_ref[D_SG:, :], preferred_element_type=F32)
    return _layer_norm(DN_ALPHA * x_ref[rows, :] + y, g_ref[...], b_ref[...])


def _sub_rows(ref):
    return [pl.ds(i * SUB, SUB) for i in range(ref.shape[0] // SUB)]


def _oproj_body(a_ref, m_ref, x_ref, ogm_ref, wo_ref, g_ref, b_ref, x1_ref):
    for rows in _sub_rows(x_ref):
        x1_ref[rows, :] = _mix_out(rows, a_ref, m_ref, x_ref, ogm_ref, wo_ref, g_ref, b_ref)


def _oproj_route_body(a_ref, m_ref, x_ref, ogm_ref, wo_ref, g_ref, b_ref, wrh_ref, wrl_ref,
                      x1_ref, route_ref, cnt_ref, cnt_sc):
    @pl.when(pl.program_id(0) == 0)
    def _():
        cnt_sc[...] = jnp.zeros(cnt_sc.shape, F32)

    lane = lax.broadcasted_iota(jnp.int32, (SUB, LANES), 1)
    lane_f = lane.astype(F32)
    r_idx = lax.broadcasted_iota(jnp.int32, (SUB, SUB), 0)
    c_idx = lax.broadcasted_iota(jnp.int32, (SUB, SUB), 1)
    earlier = jnp.where(c_idx < r_idx, 1.0, 0.0).astype(BF16)
    for rows in _sub_rows(x_ref):
        x1 = _mix_out(rows, a_ref, m_ref, x_ref, ogm_ref, wo_ref, g_ref, b_ref)
        x1_ref[rows, :] = x1
        x_hi = x1.astype(BF16)
        x_lo = (x1 - x_hi.astype(F32)).astype(BF16)
        logits = (jnp.dot(x_hi, wrh_ref[...], preferred_element_type=F32)
                  + jnp.dot(x_lo, wrh_ref[...], preferred_element_type=F32)
                  + jnp.dot(x_hi, wrl_ref[...], preferred_element_type=F32))
        lg = jnp.where(lane < N_EXPERTS, logits, -jnp.inf)
        m1 = jnp.max(lg, axis=1, keepdims=True)
        i1 = jnp.min(jnp.where(lg == m1, lane_f, float(LANES)), axis=1, keepdims=True)
        lg2 = jnp.where(lane_f == i1, -jnp.inf, lg)
        m2 = jnp.max(lg2, axis=1, keepdims=True)
        i2 = jnp.min(jnp.where(lg2 == m2, lane_f, float(LANES)), axis=1, keepdims=True)
        e = jnp.exp(m2 - m1)
        w1 = 1.0 / (1.0 + e)
        w2 = e / (1.0 + e)
        first_lo = i1 < i2
        e_lo = jnp.where(first_lo, i1, i2)
        e_hi = jnp.where(first_lo, i2, i1)
        w_lo = jnp.where(first_lo, w1, w2)
        w_hi = jnp.where(first_lo, w2, w1)

        is_lo = lane_f == e_lo
        is_hi = lane_f == e_hi
        onehot = jnp.where(is_lo | is_hi, 1.0, 0.0).astype(BF16)
        rank = jnp.dot(earlier, onehot, preferred_element_type=F32) + cnt_sc[0:1, :]
        rank_lo = jnp.sum(jnp.where(is_lo, rank, 0.0), axis=1, keepdims=True)
        rank_hi = jnp.sum(jnp.where(is_hi, rank, 0.0), axis=1, keepdims=True)
        cnt_sc[...] += jnp.dot(jnp.ones((SUBLANES, SUB), BF16), onehot,
                               preferred_element_type=F32)
        route = jnp.zeros((SUB, LANES), F32)
        for idx, colv in enumerate((e_lo, e_hi, w_lo, w_hi, rank_lo, rank_hi)):
            route = jnp.where(lane == idx, colv, route)
        route_ref[rows, :] = route
    cnt_ref[...] = cnt_sc[...]


def _oproj_call(a, m, x, ogm, wo, g, b, wr=None):
    n = x.shape[0]
    in_specs = [_row_spec(TM, D_SG), _row_spec(TM, D_ATT), _row_spec(TM, D_MODEL),
                _const_spec(ogm.shape), _const_spec(wo.shape), _const_spec(g.shape),
                _const_spec(b.shape)]
    if wr is None:
        return pl.pallas_call(
            _oproj_body, grid=(n // TM,), in_specs=in_specs,
            out_specs=_row_spec(TM, D_MODEL),
            out_shape=jax.ShapeDtypeStruct((n, D_MODEL), F32),
            compiler_params=_params("parallel"), name="oproj",
        )(a, m, x, ogm, wo, g, b)
    wr_hi = wr.astype(BF16)
    wr_lo = (wr - wr_hi.astype(F32)).astype(BF16)
    return pl.pallas_call(
        _oproj_route_body, grid=(n // TM,),
        in_specs=in_specs + [_const_spec(wr_hi.shape), _const_spec(wr_lo.shape)],
        out_specs=[_row_spec(TM, D_MODEL), _row_spec(TM, LANES),
                   pl.BlockSpec((SUBLANES, LANES), lambda i: (0, 0))],
        out_shape=[jax.ShapeDtypeStruct((n, D_MODEL), F32),
                   jax.ShapeDtypeStruct((n, LANES), F32),
                   jax.ShapeDtypeStruct((SUBLANES, LANES), F32)],
        scratch_shapes=[pltpu.VMEM((SUBLANES, LANES), F32)],
        compiler_params=_params("arbitrary"), name="oproj_route",
    )(a, m, x, ogm, wo, g, b, wr_hi, wr_lo)


def _ffn_body(x_ref, w1_ref, w3_ref, w2_ref, f_ref):
    xb = x_ref[...].astype(BF16)
    acc = jnp.zeros(f_ref.shape, F32)
    for c in range(D_FF // TF_DENSE):
        sl = slice(c * TF_DENSE, (c + 1) * TF_DENSE)
        g = jnp.dot(xb, w1_ref[:, sl], preferred_element_type=F32)
        u = jnp.dot(xb, w3_ref[:, sl], preferred_element_type=F32)
        acc = acc + jnp.dot((_silu(g) * u).astype(BF16), w2_ref[sl, :],
                            preferred_element_type=F32)
    f_ref[...] = acc


def _ffn_call(x1, w1, w3, w2):
    n = x1.shape[0]
    return pl.pallas_call(
        _ffn_body, grid=(n // TM,),
        in_specs=[_row_spec(TM, D_MODEL), _const_spec(w1.shape), _const_spec(w3.shape),
                  _const_spec(w2.shape)],
        out_specs=_row_spec(TM, D_MODEL),
        out_shape=jax.ShapeDtypeStruct((n, D_MODEL), F32),
        compiler_params=_params("parallel"), name="ffn_dense",
    )(x1, w1, w3, w2)


def _moe_body(te_ref, tv_ref, tf_ref, src0_ref, src1_ref, dst_ref, x_hbm, w1_hbm, w3_hbm, w2_hbm,
              y_hbm, xbuf, xb_sc, ybuf, wb1, wb3, wb2, st1, st3, st2, gsem, ssem, wsem):
    t = pl.program_id(0)
    tm = xb_sc.shape[0]
    n_chunks = wb1.shape[0]
    rows_per_chunk = -(-tm // (n_chunks - 2))
    expert = te_ref[t]

    def start_gather(src_ref, lo, hi):
        for r in range(lo, hi) if isinstance(lo, int) else (lo,):
            pltpu.make_async_copy(x_hbm.at[pl.ds(src_ref[0, 0, r], 1), :],
                                  xbuf.at[pl.ds(r, 1), :], gsem).start()

    def start_scatter(lo, hi):
        for r in range(lo, hi) if isinstance(lo, int) else (lo,):
            pltpu.make_async_copy(ybuf.at[pl.ds(r, 1), :],
                                  y_hbm.at[pl.ds(dst_ref[0, 0, r], 1), :], ssem).start()

    def start_rolled(start_rows):
        def body(r, carry):
            start_rows(r, r + 1)
            return carry
        lax.fori_loop(0, tm, body, 0)

    def wait_gather():
        pltpu.make_async_copy(x_hbm.at[pl.ds(0, tm), :], xbuf.at[pl.ds(0, tm), :], gsem).wait()

    def wait_scatter():
        pltpu.make_async_copy(ybuf.at[pl.ds(0, tm), :], y_hbm.at[pl.ds(0, tm), :], ssem).wait()

    def weight_copies(c, slot):
        sl = pl.ds(c * TF_MOE, TF_MOE)
        return (pltpu.make_async_copy(w1_hbm.at[expert, :, sl], st1.at[slot], wsem.at[slot]),
                pltpu.make_async_copy(w3_hbm.at[expert, :, sl], st3.at[slot], wsem.at[slot]),
                pltpu.make_async_copy(w2_hbm.at[expert, sl, :], st2.at[slot], wsem.at[slot]))

    def row_copies_after(c, part):
        pin = part[0:SUBLANES, 0:LANES]
        xbuf[tm:tm + SUBLANES, 0:LANES] = pin
        ybuf[tm:tm + SUBLANES, 0:LANES] = pin
        lo, hi = min(c * rows_per_chunk, tm), min((c + 1) * rows_per_chunk, tm)
        start_gather(src1_ref, lo, hi)
        start_scatter(lo, hi)

    def swiglu_tile(load_weights):
        xb = xb_sc[...]
        if load_weights:
            for c in range(2):
                for cp in weight_copies(c, c):
                    cp.start()
        acc = None
        for c in range(n_chunks):
            if load_weights:
                slot = c % 2
                for cp in weight_copies(c, slot):
                    cp.wait()
                w1c = st1[slot].astype(BF16)
                w3c = st3[slot].astype(BF16)
                w2c = st2[slot].astype(BF16)
                wb1[c] = w1c
                wb3[c] = w3c
                wb2[c] = w2c
                if c + 2 < n_chunks:
                    for cp in weight_copies(c + 2, slot):
                        cp.start()
            else:
                w1c, w3c, w2c = wb1[c], wb3[c], wb2[c]
            g = jnp.dot(xb, w1c, preferred_element_type=F32)
            u = jnp.dot(xb, w3c, preferred_element_type=F32)
            part = jnp.dot((_silu(g) * u).astype(BF16), w2c, preferred_element_type=F32)
            acc = part if acc is None else acc + part
            row_copies_after(c, part)
        wait_scatter()
        ybuf[0:tm, :] = acc

    @pl.when(t == 0)
    def _():
        ybuf[...] = jnp.zeros(ybuf.shape, F32)
        start_rolled(functools.partial(start_gather, src0_ref))

    wait_gather()
    xb_sc[...] = xbuf[0:tm, :].astype(BF16)
    valid = tv_ref[t] == 1
    first = tf_ref[t] == 1

    @pl.when(valid & first)
    def _():
        swiglu_tile(True)

    @pl.when(valid & jnp.logical_not(first))
    def _():
        swiglu_tile(False)

    @pl.when(jnp.logical_not(valid))
    def _():
        start_rolled(functools.partial(start_gather, src1_ref))
        start_rolled(start_scatter)
        wait_scatter()

    @pl.when(t == pl.num_programs(0) - 1)
    def _():
        wait_gather()


def _moe_call(tile_expert, tile_valid, tile_first, src_rows, dst_rows, x1, w1, w3, w2):
    n = x1.shape[0]
    n_steps = dst_rows.shape[0]
    n_chunks = D_FF_EXPERT // TF_MOE

    def row_block(offset):
        return pl.BlockSpec((1, 1, TM_MOE), lambda t, te, tv, tf: (t + offset, 0, 0),
                            memory_space=pltpu.SMEM)

    hbm = pl.BlockSpec(memory_space=pl.ANY)
    return pl.pallas_call(
        _moe_body,
        grid_spec=pltpu.PrefetchScalarGridSpec(
            num_scalar_prefetch=3,
            grid=(n_steps,),
            in_specs=[row_block(0), row_block(1), row_block(0), hbm, hbm, hbm, hbm],
            out_specs=hbm,
            scratch_shapes=[pltpu.VMEM((TM_MOE + SUBLANES, D_MODEL), F32),
                            pltpu.VMEM((TM_MOE, D_MODEL), BF16),
                            pltpu.VMEM((TM_MOE + SUBLANES, D_MODEL), F32),
                            pltpu.VMEM((n_chunks, D_MODEL, TF_MOE), BF16),
                            pltpu.VMEM((n_chunks, D_MODEL, TF_MOE), BF16),
                            pltpu.VMEM((n_chunks, TF_MOE, D_MODEL), BF16),
                            pltpu.VMEM((2, D_MODEL, TF_MOE), F32),
                            pltpu.VMEM((2, D_MODEL, TF_MOE), F32),
                            pltpu.VMEM((2, TF_MOE, D_MODEL), F32),
                            pltpu.SemaphoreType.DMA, pltpu.SemaphoreType.DMA,
                            pltpu.SemaphoreType.DMA((2,))],
        ),
        out_shape=jax.ShapeDtypeStruct((2 * n + TM_MOE, D_MODEL), F32),
        compiler_params=_params("arbitrary"), name="moe_experts",
    )(tile_expert, tile_valid, tile_first, src_rows, src_rows, dst_rows, x1, w1, w3, w2)


def _route_plan(route, counts):
    n = route.shape[0]
    n_tiles = 2 * n // TM_MOE + N_EXPERTS - 1
    cnt = counts[0, :N_EXPERTS].astype(jnp.int32)
    tiles = (cnt + TM_MOE - 1) // TM_MOE
    tile_end = jnp.cumsum(tiles)
    row0 = (tile_end - tiles) * TM_MOE
    e_pair = route[:, 0:2].astype(jnp.int32)
    rank = route[:, 4:6].astype(jnp.int32)
    dest = jnp.sum(jnp.where(e_pair[:, :, None] == jnp.arange(N_EXPERTS), row0, 0), axis=2) + rank
    tok = jnp.arange(n, dtype=jnp.int32)
    spare = 2 * n + jnp.arange(n_tiles * TM_MOE, dtype=jnp.int32) % TM_MOE
    dst_rows = spare.at[dest[:, 0]].set(tok, unique_indices=True)
    dst_rows = dst_rows.at[dest[:, 1]].set(n + tok, unique_indices=True)
    src_rows = jnp.where(dst_rows < 2 * n, dst_rows % n, dst_rows - 2 * n)
    dst_rows = jnp.concatenate([spare[:TM_MOE], dst_rows])
    src_rows = jnp.concatenate([src_rows, jnp.zeros((2 * TM_MOE,), jnp.int32)])
    t_idx = jnp.arange(n_tiles + 1)
    tile_valid = (t_idx < tile_end[-1]).astype(jnp.int32)
    tile_expert = jnp.minimum(jnp.sum(t_idx[:, None] >= tile_end[None, :], axis=1),
                              N_EXPERTS - 1).astype(jnp.int32)
    tile_first = jnp.concatenate([jnp.ones((1,), jnp.int32),
                                  (tile_expert[1:] != tile_expert[:-1]).astype(jnp.int32)])
    return (tile_expert, tile_valid, tile_first, src_rows.reshape(n_tiles + 2, 1, TM_MOE),
            dst_rows.reshape(n_tiles + 1, 1, TM_MOE))


def _ple_tail(rows, x1, f, p_ref, g_ref, b_ref, wg_ref, bg_ref, wp_ref, o_ref):
    x2 = _layer_norm(DN_ALPHA * x1 + f, g_ref[...], b_ref[...])
    gate = jax.nn.sigmoid(
        jnp.dot(x2.astype(BF16), wg_ref[...], preferred_element_type=F32) + bg_ref[...])
    emb = jnp.dot(p_ref[rows, :].astype(BF16), wp_ref[...], preferred_element_type=F32)
    o_ref[rows, :] = x2 + gate * emb


def _ple_body(x1_ref, f_ref, *rest):
    for rows in _sub_rows(x1_ref):
        _ple_tail(rows, x1_ref[rows, :], f_ref[rows, :], *rest)


def _ple_moe_body(x1_ref, ylo_ref, yhi_ref, route_ref, *rest):
    for rows in _sub_rows(x1_ref):
        route = route_ref[rows, :]
        f = route[:, 2:3] * ylo_ref[rows, :] + route[:, 3:4] * yhi_ref[rows, :]
        _ple_tail(rows, x1_ref[rows, :], f, *rest)


def _ple_call(x1, f, p, g, b, wg, bg, wp, route=None):
    n = x1.shape[0]
    tail_specs = [_row_spec(TM, D_PLE), _const_spec(g.shape), _const_spec(b.shape),
                  _const_spec(wg.shape), _const_spec(bg.shape), _const_spec(wp.shape)]
    if route is None:
        body, lead, lead_specs = _ple_body, (x1, f), [_row_spec(TM, D_MODEL)] * 2
    else:
        hi_spec = pl.BlockSpec((TM, D_MODEL), lambda i: (i + n // TM, 0))
        body, lead = _ple_moe_body, (x1, f, f, route)
        lead_specs = [_row_spec(TM, D_MODEL), _row_spec(TM, D_MODEL), hi_spec,
                      _row_spec(TM, LANES)]
    return pl.pallas_call(
        body, grid=(n // TM,),
        in_specs=lead_specs + tail_specs,
        out_specs=_row_spec(TM, D_MODEL),
        out_shape=jax.ShapeDtypeStruct((n, D_MODEL), F32),
        compiler_params=_params("parallel"), name="ln_ple",
    )(*lead, p, g, b, wg, bg, wp)


def _rope_tables(positions):
    inv_freq = 1.0 / (ROPE_THETA ** (jnp.arange(0, QK_ROPE, 2, dtype=F32) / QK_ROPE))
    ang = positions.astype(F32).reshape(-1, 1) * inv_freq
    cos, sin = jnp.cos(ang), jnp.sin(ang)
    n = cos.shape[0]
    ones = jnp.ones((n, ROPE_LANE0), F32)
    zeros = jnp.zeros((n, ROPE_LANE0), F32)
    tail1 = jnp.ones((n, LANES - ROPE_LANE0 - QK_ROPE), F32)
    tail0 = jnp.zeros((n, LANES - ROPE_LANE0 - QK_ROPE), F32)
    rc = jnp.concatenate([ones, cos, cos, tail1], axis=1)
    rs = jnp.concatenate([zeros, -sin, sin, tail0], axis=1)
    return rc, rs


def _row(v):
    return v.reshape(1, -1).astype(F32)


def kernel(x, p, positions, w_in, sg_v_g, sg_v_b, sg_w_s, sg_b_s, q_norm_g, kv_norm_g, w_uq, w_ukv, out_g, w_o, ln1_g, ln1_b, ffn_w1, ffn_w3, ffn_w2, moe_w_r, moe_w1, moe_w3, moe_w2, ln2_g, ln2_b, ple_w_g, ple_b_g, ple_w_p):
    bsz, seq, _ = x.shape
    n = bsz * seq
    xf = x.reshape(n, D_MODEL)
    rc, rs = _rope_tables(positions)

    for i in range(DEPTH):
        win = jnp.concatenate(
            [w_in[i, :, :C_KR0], jnp.zeros((D_MODEL, ROPE_LANE0), F32), w_in[i, :, C_KR0:],
             jnp.zeros((D_MODEL, LANES - ROPE_LANE0 - QK_ROPE), F32)], axis=1).astype(BF16)
        wuq = jnp.pad(w_uq[i].reshape(Q_LORA, N_HEADS, QK_NOPE + QK_ROPE),
                      ((0, 0), (0, 0), (0, HEAD_PAD - QK_NOPE - QK_ROPE)))
        wuq = wuq.reshape(Q_LORA, D_QK_PAD).astype(BF16)
        wukv = w_ukv[i].reshape(KV_LORA, N_HEADS, QK_NOPE + V_DIM)
        wuk = jnp.pad(wukv[:, :, :QK_NOPE], ((0, 0), (0, 0), (0, HEAD_PAD - QK_NOPE)))
        wuk = wuk.reshape(KV_LORA, D_QK_PAD).astype(BF16)
        wuv = jnp.pad(wukv[:, :, QK_NOPE:], ((0, 0), (0, 0), (0, HEAD_PAD - V_DIM)))
        wuv = wuv.reshape(KV_LORA, D_QK_PAD).astype(BF16)
        bexp = jnp.repeat(sg_b_s[i].T, SG_DIM, axis=1)
        a, q, k, v = _proj_call(xf, win, _row(sg_v_g[i]), _row(sg_v_b[i]), sg_w_s[i], bexp,
                                _row(q_norm_g[i]), _row(kv_norm_g[i]), wuq, wuk, wuv, rc, rs,
                                _row(out_g[i, :D_SG]))
        m = _attn_call(q.reshape(bsz, seq, D_QK_PAD), k.reshape(bsz, seq, D_QK_PAD),
                       v.reshape(bsz, seq, D_QK_PAD)).reshape(n, D_ATT)

        j = i // 2
        mix_args = (a, m, xf, _row(out_g[i, D_SG:]), w_o[i].astype(BF16), _row(ln1_g[i]),
                    _row(ln1_b[i]))
        if i % 2 == 0:
            x1 = _oproj_call(*mix_args)
            f = _ffn_call(x1, ffn_w1[j].astype(BF16), ffn_w3[j].astype(BF16),
                          ffn_w2[j].astype(BF16))
            route = None
        else:
            wr = jnp.pad(moe_w_r[j], ((0, 0), (0, LANES - N_EXPERTS)))
            x1, route, counts = _oproj_call(*mix_args, wr=wr)
            f = _moe_call(*_route_plan(route, counts), x1, moe_w1[j], moe_w3[j], moe_w2[j])

        xf = _ple_call(x1, f, p[i].reshape(n, D_PLE), _row(ln2_g[i]), _row(ln2_b[i]),
                       ple_w_g[i].astype(BF16), _row(ple_b_g[i]), ple_w_p[i].astype(BF16),
                       route=route)

    return xf.reshape(bsz, seq, D_MODEL)
```

```python
import functools
import math

import jax
import jax.numpy as jnp
from jax import lax
from jax.experimental import pallas as pl
from jax.experimental.pallas import tpu as pltpu

F32 = jnp.float32
BF16 = jnp.bfloat16

D_MODEL = 1024
DEPTH = 4
D_PLE = 256
N_SG = 8
SG_DIM = 64
D_SG = N_SG * SG_DIM
CHUNK = 128
N_HEADS = 8
QK_NOPE = 64
QK_ROPE = 32
V_DIM = 64
Q_LORA = 256
KV_LORA = 128
D_ATT = N_HEADS * V_DIM
ROPE_THETA = 10000.0
D_FF = 2816
N_EXPERTS = 8
D_FF_EXPERT = 3584
DN_ALPHA = (2.0 * DEPTH) ** 0.25
EPS = 1e-6
ATT_SCALE = (QK_NOPE + QK_ROPE) ** -0.5 * math.log2(math.e)

LANES = 128
SUBLANES = 8
HEAD_PAD = LANES
D_QK_PAD = N_HEADS * HEAD_PAD
C_Q0 = 2 * D_SG
C_KV0 = C_Q0 + Q_LORA
C_KR0 = C_KV0 + KV_LORA
D_IN_PAD = C_KR0 + LANES
ROPE_LANE0 = QK_NOPE
ROPE_HALF = QK_ROPE // 2

TM = 512
SUB = 256
TQ = 512
TF_DENSE = 256
TF_MOE = 256
TM_MOE = 512
VMEM_LIMIT = 56 * 1024 * 1024


def _params(*sem):
    return pltpu.CompilerParams(dimension_semantics=sem, vmem_limit_bytes=VMEM_LIMIT)


def _const_spec(shape):
    nd = len(shape)
    return pl.BlockSpec(shape, lambda *_: (0,) * nd, pipeline_mode=pl.Buffered(1))


def _row_spec(tm, width):
    return pl.BlockSpec((tm, width), lambda i: (i, 0))


def _layer_norm(x, g, b):
    mu = jnp.mean(x, axis=-1, keepdims=True)
    xc = x - mu
    var = jnp.mean(xc * xc, axis=-1, keepdims=True)
    return xc * lax.rsqrt(var + EPS) * g + b


def _rms_norm(x, g):
    return x * lax.rsqrt(jnp.mean(x * x, axis=-1, keepdims=True) + EPS) * g


def _gelu(x):
    return 0.5 * x * (1.0 + lax.erf(x * (1.0 / math.sqrt(2.0))))


def _silu(x):
    return x * jax.nn.sigmoid(x)


def _rope_tile(blk, rc, rs, up):
    nxt = pltpu.roll(blk, LANES - ROPE_HALF, 1)
    prv = pltpu.roll(blk, ROPE_HALF, 1)
    return blk * rc + jnp.where(up, nxt, prv) * rs


def _proj_body(x_ref, win_ref, vg_ref, vb_ref, ws_ref, bexp_ref, qg_ref, kvg_ref,
               wuq_ref, wuk_ref, wuv_ref, rc_ref, rs_ref, og_ref,
               a_ref, q_ref, k_ref, v_ref):
    tm = x_ref.shape[0]
    t_idx = lax.broadcasted_iota(jnp.int32, (CHUNK, CHUNK), 0)
    s_idx = lax.broadcasted_iota(jnp.int32, (CHUNK, CHUNK), 1)
    causal = s_idx <= t_idx
    first_group = s_idx < SG_DIM
    wpair = [
        jnp.concatenate([jnp.where(causal, ws_ref[2 * j], 0.0),
                         jnp.where(causal, ws_ref[2 * j + 1], 0.0)], axis=0).astype(BF16)
        for j in range(N_SG // 2)
    ]
    lane = lax.broadcasted_iota(jnp.int32, (SUB, LANES), 1)
    up = lane < ROPE_LANE0 + ROPE_HALF
    value_lanes = lane < V_DIM

    def in_proj(rows):
        return jnp.dot(x_ref[rows, :].astype(BF16), win_ref[...], preferred_element_type=F32)

    def rest(rows, h):
        uv = _gelu(h[:, :C_Q0])
        u = uv[:, :D_SG]
        v = _layer_norm(uv[:, D_SG:], vg_ref[...], vb_ref[...]).astype(BF16)
        mixed = []
        for c in range(SUB // CHUNK):
            blks = []
            for j in range(N_SG // 2):
                vblk = v[c * CHUNK:(c + 1) * CHUNK, j * LANES:(j + 1) * LANES]
                mm = jnp.dot(wpair[j], vblk, preferred_element_type=F32)
                blks.append(jnp.where(first_group, mm[:CHUNK], mm[CHUNK:]))
            mixed.append(jnp.concatenate(blks, axis=1) + bexp_ref[...])
        a_ref[rows, :] = _rms_norm(u * jnp.concatenate(mixed, axis=0), og_ref[...]).astype(BF16)

        rc = rc_ref[rows, :]
        rs = rs_ref[rows, :]
        cq = _rms_norm(h[:, C_Q0:C_KV0], qg_ref[...]).astype(BF16)
        q = jnp.dot(cq, wuq_ref[...], preferred_element_type=F32)
        for hd in range(N_HEADS):
            sl = slice(hd * HEAD_PAD, (hd + 1) * HEAD_PAD)
            q_ref[rows, sl] = (_rope_tile(q[:, sl], rc, rs, up) * ATT_SCALE).astype(BF16)
        ckv = _rms_norm(h[:, C_KV0:C_KR0], kvg_ref[...]).astype(BF16)
        kn = jnp.dot(ckv, wuk_ref[...], preferred_element_type=F32)
        kr = _rope_tile(h[:, C_KR0:], rc, rs, up)
        for hd in range(N_HEADS):
            sl = slice(hd * HEAD_PAD, (hd + 1) * HEAD_PAD)
            k_ref[rows, sl] = (kn[:, sl] + kr).astype(BF16)
        vv = jnp.dot(ckv, wuv_ref[...], preferred_element_type=F32)
        for hd in range(N_HEADS):
            sl = slice(hd * HEAD_PAD, (hd + 1) * HEAD_PAD)
            v_ref[rows, sl] = jnp.where(value_lanes, vv[:, sl], 1.0).astype(BF16)

    subs = [pl.ds(i * SUB, SUB) for i in range(tm // SUB)]
    h_next = in_proj(subs[0])
    for i, rows in enumerate(subs):
        h = h_next
        if i + 1 < len(subs):
            h_next = in_proj(subs[i + 1])
        rest(rows, h)


def _proj_call(x, win, vg, vb, ws, bexp, qg, kvg, wuq, wuk, wuv, rc, rs, og):
    n = x.shape[0]
    return pl.pallas_call(
        _proj_body,
        grid=(n // TM,),
        in_specs=[
            _row_spec(TM, D_MODEL),
            _const_spec(win.shape), _const_spec(vg.shape), _const_spec(vb.shape),
            _const_spec(ws.shape), _const_spec(bexp.shape), _const_spec(qg.shape),
            _const_spec(kvg.shape), _const_spec(wuq.shape), _const_spec(wuk.shape),
            _const_spec(wuv.shape),
            _row_spec(TM, LANES), _row_spec(TM, LANES),
            _const_spec(og.shape),
        ],
        out_specs=[_row_spec(TM, D_SG)] + [_row_spec(TM, D_QK_PAD)] * 3,
        out_shape=[jax.ShapeDtypeStruct((n, D_SG), BF16)]
        + [jax.ShapeDtypeStruct((n, D_QK_PAD), BF16)] * 3,
        compiler_params=_params("parallel"),
        name="proj",
    )(x, win, vg, vb, ws, bexp, qg, kvg, wuq, wuk, wuv, rc, rs, og)


def _attn_body(q_ref, k_ref, v_ref, o_ref, m_sc, acc_sc):
    tq = q_ref.shape[1]
    qi = pl.program_id(1)

    def step(off, mask):
        def scores(hh):
            sl = slice(hh * HEAD_PAD, (hh + 1) * HEAD_PAD)
            return lax.dot_general(q_ref[0, :, sl], k_ref[0, pl.ds(off, tq), sl],
                                   (((1,), (1,)), ((), ())), preferred_element_type=F32)

        ahead = 2
        s_queue = [scores(hh) for hh in range(ahead)]
        for hh in range(N_HEADS):
            sl = slice(hh * HEAD_PAD, (hh + 1) * HEAD_PAD)
            s = s_queue.pop(0)
            if hh + ahead < N_HEADS:
                s_queue.append(scores(hh + ahead))
            vblk = v_ref[0, pl.ds(off, tq), sl]
            if mask is not None:
                s = jnp.where(mask, s, -jnp.inf)
                m_next = jnp.broadcast_to(jnp.max(s, axis=1, keepdims=True), (tq, LANES))
                p = jnp.exp2(s - jnp.concatenate([m_next] * (tq // LANES), axis=1)).astype(BF16)
                acc_sc[hh] = jnp.dot(p, vblk, preferred_element_type=F32)
            else:
                m_prev = m_sc[hh]
                m_next = jnp.maximum(m_prev, jnp.max(s, axis=1, keepdims=True))
                p = jnp.exp2(s - jnp.concatenate([m_next] * (tq // LANES), axis=1)).astype(BF16)
                acc_sc[hh] = jnp.exp2(m_prev - m_next) * acc_sc[hh] + jnp.dot(
                    p, vblk, preferred_element_type=F32)
            m_sc[hh] = m_next

    row = lax.broadcasted_iota(jnp.int32, (tq, tq), 0)
    col = lax.broadcasted_iota(jnp.int32, (tq, tq), 1)
    step(pl.multiple_of(qi * tq, tq), col <= row)

    def full_block(ki, carry):
        step(pl.multiple_of(ki * tq, tq), None)
        return carry

    lax.fori_loop(0, qi, full_block, 0)

    value_lanes = lax.broadcasted_iota(jnp.int32, (tq, LANES), 1) < V_DIM
    outs = []
    for hh in range(N_HEADS):
        acc = acc_sc[hh]
        row_sum = jnp.where(value_lanes, pltpu.roll(acc, V_DIM, 1), 1.0)
        outs.append(acc / row_sum)
    for pair in range(N_HEADS // 2):
        o_ref[0, :, pair * LANES:(pair + 1) * LANES] = jnp.where(
            value_lanes, outs[2 * pair], pltpu.roll(outs[2 * pair + 1], V_DIM, 1))


def _attn_call(q, k, v):
    b, s, _ = q.shape
    kv_spec = pl.BlockSpec((1, s, D_QK_PAD), lambda bi, qi: (bi, 0, 0))
    return pl.pallas_call(
        _attn_body,
        grid=(b, s // TQ),
        in_specs=[pl.BlockSpec((1, TQ, D_QK_PAD), lambda bi, qi: (bi, qi, 0)), kv_spec, kv_spec],
        out_specs=pl.BlockSpec((1, TQ, D_ATT), lambda bi, qi: (bi, qi, 0)),
        out_shape=jax.ShapeDtypeStruct((b, s, D_ATT), F32),
        scratch_shapes=[pltpu.VMEM((N_HEADS, TQ, LANES), F32)] * 2,
        compiler_params=_params("parallel", "arbitrary"),
        name="attn",
    )(q, k, v)


def _mix_out(rows, a_ref, m_ref, x_ref, ogm_ref, wo_ref, g_ref, b_ref):
    mn = _rms_norm(m_ref[rows, :], ogm_ref[...]).astype(BF16)
    y = jnp.dot(a_ref[rows, :], wo_ref[:D_SG, :], preferred_element_type=F32)
    y = y + jnp.dot(mn, wo_ref[D_SG:, :], preferred_element_type=F32)
    return _layer_norm(DN_ALPHA * x_ref[rows, :] + y, g_ref[...], b_ref[...])


def _sub_rows(ref):
    return [pl.ds(i * SUB, SUB) for i in range(ref.shape[0] // SUB)]


def _oproj_body(a_ref, m_ref, x_ref, ogm_ref, wo_ref, g_ref, b_ref, x1_ref):
    for rows in _sub_rows(x_ref):
        x1_ref[rows, :] = _mix_out(rows, a_ref, m_ref, x_ref, ogm_ref, wo_ref, g_ref, b_ref)


def _oproj_route_body(a_ref, m_ref, x_ref, ogm_ref, wo_ref, g_ref, b_ref, wrh_ref, wrl_ref,
                      x1_ref, route_ref, cnt_ref, cnt_sc):
    @pl.when(pl.program_id(0) == 0)
    def _():
        cnt_sc[...] = jnp.zeros(cnt_sc.shape, F32)

    lane = lax.broadcasted_iota(jnp.int32, (SUB, LANES), 1)
    lane_f = lane.astype(F32)
    for rows in _sub_rows(x_ref):
        x1 = _mix_out(rows, a_ref, m_ref, x_ref, ogm_ref, wo_ref, g_ref, b_ref)
        x1_ref[rows, :] = x1
        x_hi = x1.astype(BF16)
        x_lo = (x1 - x_hi.astype(F32)).astype(BF16)
        logits = (jnp.dot(x_hi, wrh_ref[...], preferred_element_type=F32)
                  + jnp.dot(x_lo, wrh_ref[...], preferred_element_type=F32)
                  + jnp.dot(x_hi, wrl_ref[...], preferred_element_type=F32))
        lg = jnp.where(lane < N_EXPERTS, logits, -jnp.inf)
        m1 = jnp.max(lg, axis=1, keepdims=True)
        i1 = jnp.min(jnp.where(lg == m1, lane_f, float(LANES)), axis=1, keepdims=True)
        lg2 = jnp.where(lane_f == i1, -jnp.inf, lg)
        m2 = jnp.max(lg2, axis=1, keepdims=True)
        i2 = jnp.min(jnp.where(lg2 == m2, lane_f, float(LANES)), axis=1, keepdims=True)
        e = jnp.exp(m2 - m1)
        w1 = 1.0 / (1.0 + e)
        w2 = e / (1.0 + e)
        first_lo = i1 < i2
        e_lo = jnp.where(first_lo, i1, i2)
        e_hi = jnp.where(first_lo, i2, i1)
        w_lo = jnp.where(first_lo, w1, w2)
        w_hi = jnp.where(first_lo, w2, w1)

        onehot = jnp.where((lane_f == e_lo) | (lane_f == e_hi), 1.0, 0.0).astype(BF16)
        cnt_sc[...] += jnp.dot(jnp.ones((SUBLANES, SUB), BF16), onehot,
                               preferred_element_type=F32)
        route = jnp.zeros((SUB, LANES), F32)
        for idx, colv in enumerate((e_lo, e_hi, w_lo, w_hi)):
            route = jnp.where(lane == idx, colv, route)
        route_ref[rows, :] = route
    cnt_ref[...] = cnt_sc[...]


def _oproj_call(a, m, x, ogm, wo, g, b, wr=None):
    n = x.shape[0]
    in_specs = [_row_spec(TM, D_SG), _row_spec(TM, D_ATT), _row_spec(TM, D_MODEL),
                _const_spec(ogm.shape), _const_spec(wo.shape), _const_spec(g.shape),
                _const_spec(b.shape)]
    if wr is None:
        return pl.pallas_call(
            _oproj_body, grid=(n // TM,), in_specs=in_specs,
            out_specs=_row_spec(TM, D_MODEL),
            out_shape=jax.ShapeDtypeStruct((n, D_MODEL), F32),
            compiler_params=_params("parallel"), name="oproj",
        )(a, m, x, ogm, wo, g, b)
    wr_hi = wr.astype(BF16)
    wr_lo = (wr - wr_hi.astype(F32)).astype(BF16)
    return pl.pallas_call(
        _oproj_route_body, grid=(n // TM,),
        in_specs=in_specs + [_const_spec(wr_hi.shape), _const_spec(wr_lo.shape)],
        out_specs=[_row_spec(TM, D_MODEL), _row_spec(TM, LANES),
                   pl.BlockSpec((SUBLANES, LANES), lambda i: (0, 0))],
        out_shape=[jax.ShapeDtypeStruct((n, D_MODEL), F32),
                   jax.ShapeDtypeStruct((n, LANES), F32),
                   jax.ShapeDtypeStruct((SUBLANES, LANES), F32)],
        scratch_shapes=[pltpu.VMEM((SUBLANES, LANES), F32)],
        compiler_params=_params("arbitrary"), name="oproj_route",
    )(a, m, x, ogm, wo, g, b, wr_hi, wr_lo)


def _ffn_body(x_ref, w1_ref, w3_ref, w2_ref, f_ref):
    xb = x_ref[...].astype(BF16)
    acc = jnp.zeros(f_ref.shape, F32)
    for c in range(D_FF // TF_DENSE):
        sl = slice(c * TF_DENSE, (c + 1) * TF_DENSE)
        g = jnp.dot(xb, w1_ref[:, sl], preferred_element_type=F32)
        u = jnp.dot(xb, w3_ref[:, sl], preferred_element_type=F32)
        acc = acc + jnp.dot((_silu(g) * u).astype(BF16), w2_ref[sl, :],
                            preferred_element_type=F32)
    f_ref[...] = acc


def _ffn_call(x1, w1, w3, w2):
    n = x1.shape[0]
    return pl.pallas_call(
        _ffn_body, grid=(n // TM,),
        in_specs=[_row_spec(TM, D_MODEL), _const_spec(w1.shape), _const_spec(w3.shape),
                  _const_spec(w2.shape)],
        out_specs=_row_spec(TM, D_MODEL),
        out_shape=jax.ShapeDtypeStruct((n, D_MODEL), F32),
        compiler_params=_params("parallel"), name="ffn_dense",
    )(x1, w1, w3, w2)


def _moe_body(layer, te_ref, tv_ref, tf_ref, src0_ref, src1_ref, dst_ref, x_hbm, w1_hbm, w3_hbm,
              w2_hbm, y_hbm, xbuf, xb_sc, ybuf, wb1, wb3, wb2, st1, st3, st2, gsem, ssem, wsem):
    t = pl.program_id(0)
    tm = xb_sc.shape[0]
    n_chunks = wb1.shape[0]
    rows_per_chunk = -(-tm // (n_chunks - 2))
    expert = te_ref[t]

    def dma_thread(r):
        return r % 2 if isinstance(r, int) else 0

    def start_gather(src_ref, lo, hi):
        for r in range(lo, hi) if isinstance(lo, int) else (lo,):
            pltpu.make_async_copy(x_hbm.at[pl.ds(src_ref[0, 0, r], 1), :],
                                  xbuf.at[pl.ds(r, 1), :], gsem).start(priority=dma_thread(r))

    def start_scatter(lo, hi):
        for r in range(lo, hi) if isinstance(lo, int) else (lo,):
            pltpu.make_async_copy(ybuf.at[pl.ds(r, 1), :], y_hbm.at[pl.ds(dst_ref[0, 0, r], 1), :],
                                  ssem).start(priority=dma_thread(r))

    def start_rolled(start_rows):
        def body(r, carry):
            start_rows(r, r + 1)
            return carry
        lax.fori_loop(0, tm, body, 0)

    def wait_gather():
        pltpu.make_async_copy(x_hbm.at[pl.ds(0, tm), :], xbuf.at[pl.ds(0, tm), :], gsem).wait()

    def wait_scatter():
        pltpu.make_async_copy(ybuf.at[pl.ds(0, tm), :], y_hbm.at[pl.ds(0, tm), :], ssem).wait()

    def weight_copies(c, slot):
        sl = pl.ds(c * TF_MOE, TF_MOE)
        sem = wsem.at[slot]
        return (pltpu.make_async_copy(w1_hbm.at[layer, expert, :, sl], st1.at[slot], sem),
                pltpu.make_async_copy(w3_hbm.at[layer, expert, :, sl], st3.at[slot], sem),
                pltpu.make_async_copy(w2_hbm.at[layer, expert, sl, :], st2.at[slot], sem))

    def row_copies_after(c, part):
        pin = part[0:SUBLANES, 0:LANES]
        xbuf[tm:tm + SUBLANES, 0:LANES] = pin
        ybuf[tm:tm + SUBLANES, 0:LANES] = pin
        lo, hi = min(c * rows_per_chunk, tm), min((c + 1) * rows_per_chunk, tm)
        start_gather(src1_ref, lo, hi)
        start_scatter(lo, hi)

    def swiglu_tile(load_weights):
        xb = xb_sc[...]
        if load_weights:
            for c in range(2):
                for cp in weight_copies(c, c):
                    cp.start()
        acc = None
        for c in range(n_chunks):
            if load_weights:
                slot = c % 2
                for cp in weight_copies(c, slot):
                    cp.wait()
                w1c = st1[slot].astype(BF16)
                w3c = st3[slot].astype(BF16)
                w2c = st2[slot].astype(BF16)
                wb1[c] = w1c
                wb3[c] = w3c
                wb2[c] = w2c
                if c + 2 < n_chunks:
                    for cp in weight_copies(c + 2, slot):
                        cp.start()
            else:
                w1c, w3c, w2c = wb1[c], wb3[c], wb2[c]
            g = jnp.dot(xb, w1c, preferred_element_type=F32)
            u = jnp.dot(xb, w3c, preferred_element_type=F32)
            part = jnp.dot((_silu(g) * u).astype(BF16), w2c, preferred_element_type=F32)
            acc = part if acc is None else acc + part
            row_copies_after(c, part)
        wait_scatter()
        ybuf[0:tm, :] = acc

    @pl.when(t == 0)
    def _():
        ybuf[...] = jnp.zeros(ybuf.shape, F32)
        start_rolled(functools.partial(start_gather, src0_ref))

    wait_gather()
    xb_sc[...] = xbuf[0:tm, :].astype(BF16)
    valid = tv_ref[t] == 1
    first = tf_ref[t] == 1

    @pl.when(valid & first)
    def _():
        swiglu_tile(True)

    @pl.when(valid & jnp.logical_not(first))
    def _():
        swiglu_tile(False)

    @pl.when(jnp.logical_not(valid))
    def _():
        start_rolled(functools.partial(start_gather, src1_ref))
        start_rolled(start_scatter)
        wait_scatter()

    @pl.when(t == pl.num_programs(0) - 1)
    def _():
        wait_gather()


def _moe_call(layer, tile_expert, tile_valid, tile_first, src_rows, dst_rows, x1, w1, w3, w2):
    n = x1.shape[0]
    n_steps = dst_rows.shape[0]
    n_chunks = D_FF_EXPERT // TF_MOE

    def row_block(offset):
        return pl.BlockSpec((1, 1, TM_MOE), lambda t, te, tv, tf: (t + offset, 0, 0),
                            memory_space=pltpu.SMEM)

    hbm = pl.BlockSpec(memory_space=pl.ANY)
    return pl.pallas_call(
        functools.partial(_moe_body, layer),
        grid_spec=pltpu.PrefetchScalarGridSpec(
            num_scalar_prefetch=3,
            grid=(n_steps,),
            in_specs=[row_block(0), row_block(1), row_block(0), hbm, hbm, hbm, hbm],
            out_specs=hbm,
            scratch_shapes=[pltpu.VMEM((TM_MOE + SUBLANES, D_MODEL), F32),
                            pltpu.VMEM((TM_MOE, D_MODEL), BF16),
                            pltpu.VMEM((TM_MOE + SUBLANES, D_MODEL), F32),
                            pltpu.VMEM((n_chunks, D_MODEL, TF_MOE), BF16),
                            pltpu.VMEM((n_chunks, D_MODEL, TF_MOE), BF16),
                            pltpu.VMEM((n_chunks, TF_MOE, D_MODEL), BF16),
                            pltpu.VMEM((2, D_MODEL, TF_MOE), F32),
                            pltpu.VMEM((2, D_MODEL, TF_MOE), F32),
                            pltpu.VMEM((2, TF_MOE, D_MODEL), F32),
                            pltpu.SemaphoreType.DMA, pltpu.SemaphoreType.DMA,
                            pltpu.SemaphoreType.DMA((2,))],
        ),
        out_shape=jax.ShapeDtypeStruct((2 * n + TM_MOE, D_MODEL), F32),
        compiler_params=_params("arbitrary"), name="moe_experts",
    )(tile_expert, tile_valid, tile_first, src_rows, src_rows, dst_rows, x1, w1, w3, w2)


def _route_plan(route, counts):
    n = route.shape[0]
    n_tiles = 2 * n // TM_MOE + N_EXPERTS - 1
    cnt = counts[0, :N_EXPERTS].astype(jnp.int32)
    tiles = (cnt + TM_MOE - 1) // TM_MOE
    tile_end = jnp.cumsum(tiles)
    row0 = (tile_end - tiles) * TM_MOE
    first_asg = jnp.cumsum(cnt) - cnt
    t_idx = jnp.arange(n_tiles + 1)
    tile_valid = (t_idx < tile_end[-1]).astype(jnp.int32)
    tile_expert = jnp.minimum(jnp.sum(t_idx[:, None] >= tile_end[None, :], axis=1),
                              N_EXPERTS - 1).astype(jnp.int32)
    tile_first = jnp.concatenate([jnp.ones((1,), jnp.int32),
                                  (tile_expert[1:] != tile_expert[:-1]).astype(jnp.int32)])
    e_pair = route[:, 0:2].astype(jnp.int32)
    tok = jnp.arange(n, dtype=jnp.int32)
    keys = (e_pair * n + tok[:, None]) * 2 + jnp.arange(2, dtype=jnp.int32)[None, :]
    keys = jnp.sort(keys.reshape(-1))
    asg_row = (keys % 2) * n + (keys // 2) % n
    p_idx = jnp.arange(n_tiles * TM_MOE, dtype=jnp.int32)
    te = tile_expert[:n_tiles]
    r = p_idx - jnp.repeat(row0[te], TM_MOE)
    occupied = r < jnp.repeat(cnt[te], TM_MOE)
    asg = jnp.clip(jnp.repeat(first_asg[te], TM_MOE) + r, 0, 2 * n - 1)
    spare = 2 * n + p_idx % TM_MOE
    dst_rows = jnp.where(occupied, asg_row[asg], spare)
    src_rows = jnp.where(occupied, dst_rows % n, dst_rows - 2 * n)
    dst_rows = jnp.concatenate([spare[:TM_MOE], dst_rows])
    src_rows = jnp.concatenate([src_rows, jnp.zeros((2 * TM_MOE,), jnp.int32)])
    return (tile_expert, tile_valid, tile_first, src_rows.reshape(n_tiles + 2, 1, TM_MOE),
            dst_rows.reshape(n_tiles + 1, 1, TM_MOE))


def _ple_tail(rows, x1, f, p_ref, g_ref, b_ref, wg_ref, bg_ref, wp_ref, o_ref):
    x2 = _layer_norm(DN_ALPHA * x1 + f, g_ref[...], b_ref[...])
    gate = jax.nn.sigmoid(
        jnp.dot(x2.astype(BF16), wg_ref[...], preferred_element_type=F32) + bg_ref[...])
    emb = jnp.dot(p_ref[rows, :].astype(BF16), wp_ref[...], preferred_element_type=F32)
    o_ref[rows, :] = x2 + gate * emb


def _ple_body(x1_ref, f_ref, *rest):
    for rows in _sub_rows(x1_ref):
        _ple_tail(rows, x1_ref[rows, :], f_ref[rows, :], *rest)


def _ple_moe_body(x1_ref, ylo_ref, yhi_ref, route_ref, *rest):
    for rows in _sub_rows(x1_ref):
        route = route_ref[rows, :]
        f = route[:, 2:3] * ylo_ref[rows, :] + route[:, 3:4] * yhi_ref[rows, :]
        _ple_tail(rows, x1_ref[rows, :], f, *rest)


def _ple_call(x1, f, p, g, b, wg, bg, wp, route=None):
    n = x1.shape[0]
    tail_specs = [_row_spec(TM, D_PLE), _const_spec(g.shape), _const_spec(b.shape),
                  _const_spec(wg.shape), _const_spec(bg.shape), _const_spec(wp.shape)]
    if route is None:
        body, lead, lead_specs = _ple_body, (x1, f), [_row_spec(TM, D_MODEL)] * 2
    else:
        hi_spec = pl.BlockSpec((TM, D_MODEL), lambda i: (i + n // TM, 0))
        body, lead = _ple_moe_body, (x1, f, f, route)
        lead_specs = [_row_spec(TM, D_MODEL), _row_spec(TM, D_MODEL), hi_spec,
                      _row_spec(TM, LANES)]
    return pl.pallas_call(
        body, grid=(n // TM,),
        in_specs=lead_specs + tail_specs,
        out_specs=_row_spec(TM, D_MODEL),
        out_shape=jax.ShapeDtypeStruct((n, D_MODEL), F32),
        compiler_params=_params("parallel"), name="ln_ple",
    )(*lead, p, g, b, wg, bg, wp)


def _rope_tables(positions):
    inv_freq = 1.0 / (ROPE_THETA ** (jnp.arange(0, QK_ROPE, 2, dtype=F32) / QK_ROPE))
    ang = positions.astype(F32).reshape(-1, 1) * inv_freq
    cos, sin = jnp.cos(ang), jnp.sin(ang)
    n = cos.shape[0]
    ones = jnp.ones((n, ROPE_LANE0), F32)
    zeros = jnp.zeros((n, ROPE_LANE0), F32)
    tail1 = jnp.ones((n, LANES - ROPE_LANE0 - QK_ROPE), F32)
    tail0 = jnp.zeros((n, LANES - ROPE_LANE0 - QK_ROPE), F32)
    rc = jnp.concatenate([ones, cos, cos, tail1], axis=1)
    rs = jnp.concatenate([zeros, -sin, sin, tail0], axis=1)
    return rc, rs


def _row(v):
    return v.reshape(1, -1).astype(F32)


def kernel(x, p, positions, w_in, sg_v_g, sg_v_b, sg_w_s, sg_b_s, q_norm_g, kv_norm_g, w_uq, w_ukv, out_g, w_o, ln1_g, ln1_b, ffn_w1, ffn_w3, ffn_w2, moe_w_r, moe_w1, moe_w3, moe_w2, ln2_g, ln2_b, ple_w_g, ple_b_g, ple_w_p):
    bsz, seq, _ = x.shape
    n = bsz * seq
    xf = x.reshape(n, D_MODEL)
    rc, rs = _rope_tables(positions)

    for i in range(DEPTH):
        win = jnp.concatenate(
            [w_in[i, :, :C_KR0], jnp.zeros((D_MODEL, ROPE_LANE0), F32), w_in[i, :, C_KR0:],
             jnp.zeros((D_MODEL, LANES - ROPE_LANE0 - QK_ROPE), F32)], axis=1).astype(BF16)
        wuq = jnp.pad(w_uq[i].reshape(Q_LORA, N_HEADS, QK_NOPE + QK_ROPE),
                      ((0, 0), (0, 0), (0, HEAD_PAD - QK_NOPE - QK_ROPE)))
        wuq = wuq.reshape(Q_LORA, D_QK_PAD).astype(BF16)
        wukv = w_ukv[i].reshape(KV_LORA, N_HEADS, QK_NOPE + V_DIM)
        wuk = jnp.pad(wukv[:, :, :QK_NOPE], ((0, 0), (0, 0), (0, HEAD_PAD - QK_NOPE)))
        wuk = wuk.reshape(KV_LORA, D_QK_PAD).astype(BF16)
        wuv = jnp.pad(wukv[:, :, QK_NOPE:], ((0, 0), (0, 0), (0, HEAD_PAD - V_DIM)))
        wuv = wuv.reshape(KV_LORA, D_QK_PAD).astype(BF16)
        bexp = jnp.repeat(sg_b_s[i].T, SG_DIM, axis=1)
        a, q, k, v = _proj_call(xf, win, _row(sg_v_g[i]), _row(sg_v_b[i]), sg_w_s[i], bexp,
                                _row(q_norm_g[i]), _row(kv_norm_g[i]), wuq, wuk, wuv, rc, rs,
                                _row(out_g[i, :D_SG]))
        m = _attn_call(q.reshape(bsz, seq, D_QK_PAD), k.reshape(bsz, seq, D_QK_PAD),
                       v.reshape(bsz, seq, D_QK_PAD)).reshape(n, D_ATT)

        j = i // 2
        mix_args = (a, m, xf, _row(out_g[i, D_SG:]), w_o[i].astype(BF16), _row(ln1_g[i]),
                    _row(ln1_b[i]))
        if i % 2 == 0:
            x1 = _oproj_call(*mix_args)
            f = _ffn_call(x1, ffn_w1[j].astype(BF16), ffn_w3[j].astype(BF16),
                          ffn_w2[j].astype(BF16))
            route = None
        else:
            wr = jnp.pad(moe_w_r[j], ((0, 0), (0, LANES - N_EXPERTS)))
            x1, route, counts = _oproj_call(*mix_args, wr=wr)
            f = _moe_call(j, *_route_plan(route, counts), x1, moe_w1, moe_w3, moe_w2)

        xf = _ple_call(x1, f, p[i].reshape(n, D_PLE), _row(ln2_g[i]), _row(ln2_b[i]),
                       ple_w_g[i].astype(BF16), _row(ple_b_g[i]), ple_w_p[i].astype(BF16),
                       route=route)

    return xf.reshape(bsz, seq, D_MODEL)
```

```python
import functools
import math

import jax
import jax.numpy as jnp
from jax import lax
from jax.experimental import pallas as pl
from jax.experimental.pallas import tpu as pltpu

F32 = jnp.float32
BF16 = jnp.bfloat16

D_MODEL = 1024
DEPTH = 4
D_PLE = 256
N_SG = 8
SG_DIM = 64
D_SG = N_SG * SG_DIM
CHUNK = 128
N_HEADS = 8
QK_NOPE = 64
QK_ROPE = 32
V_DIM = 64
Q_LORA = 256
KV_LORA = 128
D_ATT = N_HEADS * V_DIM
ROPE_THETA = 10000.0
D_FF = 2816
N_EXPERTS = 8
D_FF_EXPERT = 3584
DN_ALPHA = (2.0 * DEPTH) ** 0.25
EPS = 1e-6
ATT_SCALE = (QK_NOPE + QK_ROPE) ** -0.5 * math.log2(math.e)

LANES = 128
SUBLANES = 8
HEAD_PAD = LANES
D_QK_PAD = N_HEADS * HEAD_PAD
C_Q0 = 2 * D_SG
C_KV0 = C_Q0 + Q_LORA
C_KR0 = C_KV0 + KV_LORA
D_IN_PAD = C_KR0 + LANES
ROPE_LANE0 = QK_NOPE
ROPE_HALF = QK_ROPE // 2

TM = 512
SUB = 256
TQ = 512
TF_DENSE = 256
ROW_PITCH = SUBLANES + 1
TF_MOE = 256
TM_MOE = 512
VMEM_LIMIT = 56 * 1024 * 1024


def _params(*sem):
    return pltpu.CompilerParams(dimension_semantics=sem, vmem_limit_bytes=VMEM_LIMIT)


def _const_spec(shape):
    nd = len(shape)
    return pl.BlockSpec(shape, lambda *_: (0,) * nd, pipeline_mode=pl.Buffered(1))


def _row_spec(tm, width):
    return pl.BlockSpec((tm, width), lambda i: (i, 0))


def _layer_norm(x, g, b):
    mu = jnp.mean(x, axis=-1, keepdims=True)
    xc = x - mu
    var = jnp.mean(xc * xc, axis=-1, keepdims=True)
    return xc * lax.rsqrt(var + EPS) * g + b


def _rms_norm(x, g):
    return x * lax.rsqrt(jnp.mean(x * x, axis=-1, keepdims=True) + EPS) * g


def _gelu(x):
    return 0.5 * x * (1.0 + lax.erf(x * (1.0 / math.sqrt(2.0))))


def _silu(x):
    return x * jax.nn.sigmoid(x)


def _rope_tile(blk, rc, rs, up):
    nxt = pltpu.roll(blk, LANES - ROPE_HALF, 1)
    prv = pltpu.roll(blk, ROPE_HALF, 1)
    return blk * rc + jnp.where(up, nxt, prv) * rs


def _proj_body(x_ref, win_ref, vg_ref, vb_ref, ws_ref, bexp_ref, qg_ref, kvg_ref,
               wuq_ref, wuk_ref, wuv_ref, rc_ref, rs_ref, og_ref,
               a_ref, q_ref, k_ref, v_ref):
    tm = x_ref.shape[0]
    t_idx = lax.broadcasted_iota(jnp.int32, (CHUNK, CHUNK), 0)
    s_idx = lax.broadcasted_iota(jnp.int32, (CHUNK, CHUNK), 1)
    causal = s_idx <= t_idx
    first_group = s_idx < SG_DIM
    wpair = [
        jnp.concatenate([jnp.where(causal, ws_ref[2 * j], 0.0),
                         jnp.where(causal, ws_ref[2 * j + 1], 0.0)], axis=0).astype(BF16)
        for j in range(N_SG // 2)
    ]
    lane = lax.broadcasted_iota(jnp.int32, (SUB, LANES), 1)
    up = lane < ROPE_LANE0 + ROPE_HALF
    value_lanes = lane < V_DIM

    def in_proj(rows):
        return jnp.dot(x_ref[rows, :].astype(BF16), win_ref[...], preferred_element_type=F32)

    def rest(rows, h):
        uv = _gelu(h[:, :C_Q0])
        u = uv[:, :D_SG]
        v = _layer_norm(uv[:, D_SG:], vg_ref[...], vb_ref[...]).astype(BF16)
        mixed = []
        for c in range(SUB // CHUNK):
            blks = []
            for j in range(N_SG // 2):
                vblk = v[c * CHUNK:(c + 1) * CHUNK, j * LANES:(j + 1) * LANES]
                mm = jnp.dot(wpair[j], vblk, preferred_element_type=F32)
                blks.append(jnp.where(first_group, mm[:CHUNK], mm[CHUNK:]))
            mixed.append(jnp.concatenate(blks, axis=1) + bexp_ref[...])
        a_ref[rows, :] = _rms_norm(u * jnp.concatenate(mixed, axis=0), og_ref[...]).astype(BF16)

        rc = rc_ref[rows, :]
        rs = rs_ref[rows, :]
        cq = _rms_norm(h[:, C_Q0:C_KV0], qg_ref[...]).astype(BF16)
        q = jnp.dot(cq, wuq_ref[...], preferred_element_type=F32)
        for hd in range(N_HEADS):
            sl = slice(hd * HEAD_PAD, (hd + 1) * HEAD_PAD)
            q_ref[rows, sl] = (_rope_tile(q[:, sl], rc, rs, up) * ATT_SCALE).astype(BF16)
        ckv = _rms_norm(h[:, C_KV0:C_KR0], kvg_ref[...]).astype(BF16)
        kn = jnp.dot(ckv, wuk_ref[...], preferred_element_type=F32)
        kr = _rope_tile(h[:, C_KR0:], rc, rs, up)
        for hd in range(N_HEADS):
            sl = slice(hd * HEAD_PAD, (hd + 1) * HEAD_PAD)
            k_ref[rows, sl] = (kn[:, sl] + kr).astype(BF16)
        vv = jnp.dot(ckv, wuv_ref[...], preferred_element_type=F32)
        for hd in range(N_HEADS):
            sl = slice(hd * HEAD_PAD, (hd + 1) * HEAD_PAD)
            v_ref[rows, sl] = jnp.where(value_lanes, vv[:, sl], 1.0).astype(BF16)

    subs = [pl.ds(i * SUB, SUB) for i in range(tm // SUB)]
    h_next = in_proj(subs[0])
    for i, rows in enumerate(subs):
        h = h_next
        if i + 1 < len(subs):
            h_next = in_proj(subs[i + 1])
        rest(rows, h)


def _proj_call(x, win, vg, vb, ws, bexp, qg, kvg, wuq, wuk, wuv, rc, rs, og):
    n = x.shape[0]
    return pl.pallas_call(
        _proj_body,
        grid=(n // TM,),
        in_specs=[
            _row_spec(TM, D_MODEL),
            _const_spec(win.shape), _const_spec(vg.shape), _const_spec(vb.shape),
            _const_spec(ws.shape), _const_spec(bexp.shape), _const_spec(qg.shape),
            _const_spec(kvg.shape), _const_spec(wuq.shape), _const_spec(wuk.shape),
            _const_spec(wuv.shape),
            _row_spec(TM, LANES), _row_spec(TM, LANES),
            _const_spec(og.shape),
        ],
        out_specs=[_row_spec(TM, D_SG)] + [_row_spec(TM, D_QK_PAD)] * 3,
        out_shape=[jax.ShapeDtypeStruct((n, D_SG), BF16)]
        + [jax.ShapeDtypeStruct((n, D_QK_PAD), BF16)] * 3,
        compiler_params=_params("parallel"),
        name="proj",
    )(x, win, vg, vb, ws, bexp, qg, kvg, wuq, wuk, wuv, rc, rs, og)


def _attn_body(q_ref, k_ref, v_ref, o_ref, m_sc, acc_sc):
    tq = q_ref.shape[1]
    qi = pl.program_id(1)

    def step(off, mask):
        def scores(hh):
            sl = slice(hh * HEAD_PAD, (hh + 1) * HEAD_PAD)
            return lax.dot_general(q_ref[0, :, sl], k_ref[0, pl.ds(off, tq), sl],
                                   (((1,), (1,)), ((), ())), preferred_element_type=F32)

        ahead = 2
        s_queue = [scores(hh) for hh in range(ahead)]
        for hh in range(N_HEADS):
            sl = slice(hh * HEAD_PAD, (hh + 1) * HEAD_PAD)
            s = s_queue.pop(0)
            if hh + ahead < N_HEADS:
                s_queue.append(scores(hh + ahead))
            vblk = v_ref[0, pl.ds(off, tq), sl]
            if mask is not None:
                s = jnp.where(mask, s, -jnp.inf)
                m_next = jnp.broadcast_to(jnp.max(s, axis=1, keepdims=True), (tq, LANES))
                p = jnp.exp2(s - jnp.concatenate([m_next] * (tq // LANES), axis=1)).astype(BF16)
                acc_sc[hh] = jnp.dot(p, vblk, preferred_element_type=F32)
            else:
                m_prev = m_sc[hh]
                m_next = jnp.maximum(m_prev, jnp.max(s, axis=1, keepdims=True))
                p = jnp.exp2(s - jnp.concatenate([m_next] * (tq // LANES), axis=1)).astype(BF16)
                acc_sc[hh] = jnp.exp2(m_prev - m_next) * acc_sc[hh] + jnp.dot(
                    p, vblk, preferred_element_type=F32)
            m_sc[hh] = m_next

    row = lax.broadcasted_iota(jnp.int32, (tq, tq), 0)
    col = lax.broadcasted_iota(jnp.int32, (tq, tq), 1)
    step(pl.multiple_of(qi * tq, tq), col <= row)

    def full_block(ki, carry):
        step(pl.multiple_of(ki * tq, tq), None)
        return carry

    lax.fori_loop(0, qi, full_block, 0)

    value_lanes = lax.broadcasted_iota(jnp.int32, (tq, LANES), 1) < V_DIM
    outs = []
    for hh in range(N_HEADS):
        acc = acc_sc[hh]
        row_sum = jnp.where(value_lanes, pltpu.roll(acc, V_DIM, 1), 1.0)
        outs.append(acc / row_sum)
    for pair in range(N_HEADS // 2):
        o_ref[0, :, pair * LANES:(pair + 1) * LANES] = jnp.where(
            value_lanes, outs[2 * pair], pltpu.roll(outs[2 * pair + 1], V_DIM, 1))


def _attn_call(q, k, v):
    b, s, _ = q.shape
    kv_spec = pl.BlockSpec((1, s, D_QK_PAD), lambda bi, qi: (bi, 0, 0))
    return pl.pallas_call(
        _attn_body,
        grid=(b, s // TQ),
        in_specs=[pl.BlockSpec((1, TQ, D_QK_PAD), lambda bi, qi: (bi, qi, 0)), kv_spec, kv_spec],
        out_specs=pl.BlockSpec((1, TQ, D_ATT), lambda bi, qi: (bi, qi, 0)),
        out_shape=jax.ShapeDtypeStruct((b, s, D_ATT), F32),
        scratch_shapes=[pltpu.VMEM((N_HEADS, TQ, LANES), F32)] * 2,
        compiler_params=_params("parallel", "arbitrary"),
        name="attn",
    )(q, k, v)


def _mix_out(rows, a_ref, m_ref, x_ref, ogm_ref, wo_ref, g_ref, b_ref):
    mn = _rms_norm(m_ref[rows, :], ogm_ref[...]).astype(BF16)
    y = jnp.dot(a_ref[rows, :], wo_ref[:D_SG, :], preferred_element_type=F32)
    y = y + jnp.dot(mn, wo_ref[D_SG:, :], preferred_element_type=F32)
    return _layer_norm(DN_ALPHA * x_ref[rows, :] + y, g_ref[...], b_ref[...])


def _sub_rows(ref):
    return [pl.ds(i * SUB, SUB) for i in range(ref.shape[0] // SUB)]


def _store_row_tiles(ref, t0, val):
    nt = val.shape[0]
    for j in range(D_MODEL // LANES):
        ref[pl.ds(ROW_PITCH * t0 + j, nt, stride=ROW_PITCH), :] = val[:, j * LANES:(j + 1) * LANES]
    ref[pl.ds(ROW_PITCH * t0 + SUBLANES, nt, stride=ROW_PITCH), :] = jnp.zeros((nt, LANES), F32)


def _load_row_tiles(ref, t0, nt):
    return jnp.concatenate([ref[pl.ds(ROW_PITCH * t0 + j, nt, stride=ROW_PITCH), :]
                            for j in range(D_MODEL // LANES)], axis=1)


def _oproj_route_body(a_ref, m_ref, x_ref, ogm_ref, wo_ref, g_ref, b_ref, wrh_ref, wrl_ref,
                      x1_ref, x1t_ref, route_ref, cnt_ref, cnt_sc):
    @pl.when(pl.program_id(0) == 0)
    def _():
        cnt_sc[...] = jnp.zeros(cnt_sc.shape, F32)

    lane = lax.broadcasted_iota(jnp.int32, (SUB, LANES), 1)
    lane_f = lane.astype(F32)
    for i, rows in enumerate(_sub_rows(x_ref)):
        x1 = _mix_out(rows, a_ref, m_ref, x_ref, ogm_ref, wo_ref, g_ref, b_ref)
        x1_ref[rows, :] = x1
        _store_row_tiles(x1t_ref, i * SUB, x1)
        x_hi = x1.astype(BF16)
        x_lo = (x1 - x_hi.astype(F32)).astype(BF16)
        logits = (jnp.dot(x_hi, wrh_ref[...], preferred_element_type=F32)
                  + jnp.dot(x_lo, wrh_ref[...], preferred_element_type=F32)
                  + jnp.dot(x_hi, wrl_ref[...], preferred_element_type=F32))
        lg = jnp.where(lane < N_EXPERTS, logits, -jnp.inf)
        m1 = jnp.max(lg, axis=1, keepdims=True)
        i1 = jnp.min(jnp.where(lg == m1, lane_f, float(LANES)), axis=1, keepdims=True)
        lg2 = jnp.where(lane_f == i1, -jnp.inf, lg)
        m2 = jnp.max(lg2, axis=1, keepdims=True)
        i2 = jnp.min(jnp.where(lg2 == m2, lane_f, float(LANES)), axis=1, keepdims=True)
        e = jnp.exp(m2 - m1)
        w1 = 1.0 / (1.0 + e)
        w2 = e / (1.0 + e)
        first_lo = i1 < i2
        e_lo = jnp.where(first_lo, i1, i2)
        e_hi = jnp.where(first_lo, i2, i1)
        w_lo = jnp.where(first_lo, w1, w2)
        w_hi = jnp.where(first_lo, w2, w1)

        onehot = jnp.where((lane_f == e_lo) | (lane_f == e_hi), 1.0, 0.0).astype(BF16)
        cnt_sc[...] += jnp.dot(jnp.ones((SUBLANES, SUB), BF16), onehot,
                               preferred_element_type=F32)
        route = jnp.zeros((SUB, LANES), F32)
        for idx, colv in enumerate((e_lo, e_hi, w_lo, w_hi)):
            route = jnp.where(lane == idx, colv, route)
        route_ref[rows, :] = route
    cnt_ref[...] = cnt_sc[...]


def _oproj_call(a, m, x, ogm, wo, g, b, wr):
    n = x.shape[0]
    in_specs = [_row_spec(TM, D_SG), _row_spec(TM, D_ATT), _row_spec(TM, D_MODEL),
                _const_spec(ogm.shape), _const_spec(wo.shape), _const_spec(g.shape),
                _const_spec(b.shape)]
    wr_hi = wr.astype(BF16)
    wr_lo = (wr - wr_hi.astype(F32)).astype(BF16)
    return pl.pallas_call(
        _oproj_route_body, grid=(n // TM,),
        in_specs=in_specs + [_const_spec(wr_hi.shape), _const_spec(wr_lo.shape)],
        out_specs=[_row_spec(TM, D_MODEL), _row_spec(TM * ROW_PITCH, LANES), _row_spec(TM, LANES),
                   pl.BlockSpec((SUBLANES, LANES), lambda i: (0, 0))],
        out_shape=[jax.ShapeDtypeStruct((n, D_MODEL), F32),
                   jax.ShapeDtypeStruct((n * ROW_PITCH, LANES), F32),
                   jax.ShapeDtypeStruct((n, LANES), F32),
                   jax.ShapeDtypeStruct((SUBLANES, LANES), F32)],
        scratch_shapes=[pltpu.VMEM((SUBLANES, LANES), F32)],
        compiler_params=_params("arbitrary"), name="oproj_route",
    )(a, m, x, ogm, wo, g, b, wr_hi, wr_lo)


def _dense_tail_body(a_ref, m_ref, x_ref, ogm_ref, wo_ref, g1_ref, b1_ref, w1_ref, w3_ref, w2_ref,
                     p_ref, g2_ref, b2_ref, wg_ref, bg_ref, wp_ref, o_ref, x1_sc):
    for rows in _sub_rows(x_ref):
        x1_sc[rows, :] = _mix_out(rows, a_ref, m_ref, x_ref, ogm_ref, wo_ref, g1_ref, b1_ref)
    xb = x1_sc[...].astype(BF16)
    f = None
    for c in range(D_FF // TF_DENSE):
        sl = slice(c * TF_DENSE, (c + 1) * TF_DENSE)
        g = jnp.dot(xb, w1_ref[:, sl], preferred_element_type=F32)
        u = jnp.dot(xb, w3_ref[:, sl], preferred_element_type=F32)
        part = jnp.dot((_silu(g) * u).astype(BF16), w2_ref[sl, :], preferred_element_type=F32)
        f = part if f is None else f + part
    for i, rows in enumerate(_sub_rows(x_ref)):
        _ple_tail(rows, x1_sc[rows, :], f[i * SUB:(i + 1) * SUB, :], p_ref, g2_ref, b2_ref,
                  wg_ref, bg_ref, wp_ref, o_ref)


def _dense_tail_call(a, m, x, ogm, wo, g1, b1, w1, w3, w2, p, g2, b2, wg, bg, wp):
    n = x.shape[0]
    consts = (ogm, wo, g1, b1, w1, w3, w2)
    tail_consts = (g2, b2, wg, bg, wp)
    return pl.pallas_call(
        _dense_tail_body, grid=(n // TM,),
        in_specs=[_row_spec(TM, D_SG), _row_spec(TM, D_ATT), _row_spec(TM, D_MODEL)]
        + [_const_spec(c.shape) for c in consts] + [_row_spec(TM, D_PLE)]
        + [_const_spec(c.shape) for c in tail_consts],
        out_specs=_row_spec(TM, D_MODEL),
        out_shape=jax.ShapeDtypeStruct((n, D_MODEL), F32),
        scratch_shapes=[pltpu.VMEM((TM, D_MODEL), F32)],
        compiler_params=_params("parallel"), name="dense_tail",
    )(a, m, x, *consts, p, *tail_consts)


def _moe_body(layer, te_ref, tv_ref, tf_ref, src0_ref, src1_ref, dst_ref, x_hbm, w1_hbm, w3_hbm,
              w2_hbm, y_hbm, xbuf, xb_sc, ybuf, wb1, wb3, wb2, st1, st3, st2, gsem, ssem, wsem):
    t = pl.program_id(0)
    tm = xb_sc.shape[0]
    n_chunks = wb1.shape[0]
    rows_per_chunk = -(-tm // (n_chunks - 2))
    expert = te_ref[t]
    n_tile_rows = ROW_PITCH * tm

    def dma_thread(r):
        return r % 2 if isinstance(r, int) else 0

    def start_gather(src_ref, lo, hi):
        for r in range(lo, hi) if isinstance(lo, int) else (lo,):
            pltpu.make_async_copy(x_hbm.at[pl.ds(ROW_PITCH * src_ref[0, 0, r], ROW_PITCH), :],
                                  xbuf.at[pl.ds(ROW_PITCH * r, ROW_PITCH), :],
                                  gsem).start(priority=dma_thread(r))

    def start_scatter(lo, hi):
        for r in range(lo, hi) if isinstance(lo, int) else (lo,):
            pltpu.make_async_copy(ybuf.at[pl.ds(ROW_PITCH * r, ROW_PITCH), :],
                                  y_hbm.at[pl.ds(ROW_PITCH * dst_ref[0, 0, r], ROW_PITCH), :],
                                  ssem).start(priority=dma_thread(r))

    def start_rolled(start_rows):
        def body(r, carry):
            start_rows(r, r + 1)
            return carry
        lax.fori_loop(0, tm, body, 0)

    def wait_gather():
        pltpu.make_async_copy(x_hbm.at[pl.ds(0, n_tile_rows), :], xbuf.at[pl.ds(0, n_tile_rows), :],
                              gsem).wait()

    def wait_scatter():
        pltpu.make_async_copy(ybuf.at[pl.ds(0, n_tile_rows), :], y_hbm.at[pl.ds(0, n_tile_rows), :],
                              ssem).wait()

    def weight_copies(c, slot):
        sl = pl.ds(c * TF_MOE, TF_MOE)
        sem = wsem.at[slot]
        return (pltpu.make_async_copy(w1_hbm.at[layer, expert, :, sl], st1.at[slot], sem),
                pltpu.make_async_copy(w3_hbm.at[layer, expert, :, sl], st3.at[slot], sem),
                pltpu.make_async_copy(w2_hbm.at[layer, expert, sl, :], st2.at[slot], sem))

    def row_copies_after(c, part):
        pin = part[0:SUBLANES, 0:LANES]
        xbuf[n_tile_rows:n_tile_rows + SUBLANES, :] = pin
        ybuf[n_tile_rows:n_tile_rows + SUBLANES, :] = pin
        lo, hi = min(c * rows_per_chunk, tm), min((c + 1) * rows_per_chunk, tm)
        start_gather(src1_ref, lo, hi)
        start_scatter(lo, hi)

    def swiglu_tile(load_weights):
        xb = xb_sc[...]
        if load_weights:
            for c in range(2):
                for cp in weight_copies(c, c):
                    cp.start()
        acc = None
        for c in range(n_chunks):
            if load_weights:
                slot = c % 2
                for cp in weight_copies(c, slot):
                    cp.wait()
                w1c = st1[slot].astype(BF16)
                w3c = st3[slot].astype(BF16)
                w2c = st2[slot].astype(BF16)
                wb1[c] = w1c
                wb3[c] = w3c
                wb2[c] = w2c
                if c + 2 < n_chunks:
                    for cp in weight_copies(c + 2, slot):
                        cp.start()
            else:
                w1c, w3c, w2c = wb1[c], wb3[c], wb2[c]
            g = jnp.dot(xb, w1c, preferred_element_type=F32)
            u = jnp.dot(xb, w3c, preferred_element_type=F32)
            part = jnp.dot((_silu(g) * u).astype(BF16), w2c, preferred_element_type=F32)
            acc = part if acc is None else acc + part
            row_copies_after(c, part)
        wait_scatter()
        _store_row_tiles(ybuf, 0, acc)

    @pl.when(t == 0)
    def _():
        ybuf[...] = jnp.zeros(ybuf.shape, F32)
        start_rolled(functools.partial(start_gather, src0_ref))

    wait_gather()
    xb_sc[...] = _load_row_tiles(xbuf, 0, tm).astype(BF16)
    valid = tv_ref[t] == 1
    first = tf_ref[t] == 1

    @pl.when(valid & first)
    def _():
        swiglu_tile(True)

    @pl.when(valid & jnp.logical_not(first))
    def _():
        swiglu_tile(False)

    @pl.when(jnp.logical_not(valid))
    def _():
        start_rolled(functools.partial(start_gather, src1_ref))
        start_rolled(start_scatter)
        wait_scatter()

    @pl.when(t == pl.num_programs(0) - 1)
    def _():
        wait_gather()


def _moe_call(layer, tile_expert, tile_valid, tile_first, src_rows, dst_rows, x1, w1, w3, w2):
    n = x1.shape[0] // ROW_PITCH
    n_steps = dst_rows.shape[0]
    n_chunks = D_FF_EXPERT // TF_MOE

    def row_block(offset):
        return pl.BlockSpec((1, 1, TM_MOE), lambda t, te, tv, tf: (t + offset, 0, 0),
                            memory_space=pltpu.SMEM)

    hbm = pl.BlockSpec(memory_space=pl.ANY)
    return pl.pallas_call(
        functools.partial(_moe_body, layer),
        grid_spec=pltpu.PrefetchScalarGridSpec(
            num_scalar_prefetch=3,
            grid=(n_steps,),
            in_specs=[row_block(0), row_block(1), row_block(0), hbm, hbm, hbm, hbm],
            out_specs=hbm,
            scratch_shapes=[pltpu.VMEM((TM_MOE * ROW_PITCH + SUBLANES, LANES), F32),
                            pltpu.VMEM((TM_MOE, D_MODEL), BF16),
                            pltpu.VMEM((TM_MOE * ROW_PITCH + SUBLANES, LANES), F32),
                            pltpu.VMEM((n_chunks, D_MODEL, TF_MOE), BF16),
                            pltpu.VMEM((n_chunks, D_MODEL, TF_MOE), BF16),
                            pltpu.VMEM((n_chunks, TF_MOE, D_MODEL), BF16),
                            pltpu.VMEM((2, D_MODEL, TF_MOE), F32),
                            pltpu.VMEM((2, D_MODEL, TF_MOE), F32),
                            pltpu.VMEM((2, TF_MOE, D_MODEL), F32),
                            pltpu.SemaphoreType.DMA, pltpu.SemaphoreType.DMA,
                            pltpu.SemaphoreType.DMA((2,))],
        ),
        out_shape=jax.ShapeDtypeStruct(((2 * n + TM_MOE) * ROW_PITCH, LANES), F32),
        compiler_params=_params("arbitrary"), name="moe_experts",
    )(tile_expert, tile_valid, tile_first, src_rows, src_rows, dst_rows, x1, w1, w3, w2)


def _route_plan(route, counts):
    n = route.shape[0]
    n_tiles = 2 * n // TM_MOE + N_EXPERTS - 1
    cnt = counts[0, :N_EXPERTS].astype(jnp.int32)
    tiles = (cnt + TM_MOE - 1) // TM_MOE
    tile_end = jnp.cumsum(tiles)
    row0 = (tile_end - tiles) * TM_MOE
    first_asg = jnp.cumsum(cnt) - cnt
    t_idx = jnp.arange(n_tiles + 1)
    tile_valid = (t_idx < tile_end[-1]).astype(jnp.int32)
    tile_expert = jnp.minimum(jnp.sum(t_idx[:, None] >= tile_end[None, :], axis=1),
                              N_EXPERTS - 1).astype(jnp.int32)
    tile_first = jnp.concatenate([jnp.ones((1,), jnp.int32),
                                  (tile_expert[1:] != tile_expert[:-1]).astype(jnp.int32)])
    e_pair = route[:, 0:2].astype(jnp.int32)
    tok = jnp.arange(n, dtype=jnp.int32)
    keys = (e_pair * n + tok[:, None]) * 2 + jnp.arange(2, dtype=jnp.int32)[None, :]
    keys = jnp.sort(keys.reshape(-1))
    asg_row = (keys % 2) * n + (keys // 2) % n
    p_idx = jnp.arange(n_tiles * TM_MOE, dtype=jnp.int32)
    te = tile_expert[:n_tiles]
    r = p_idx - jnp.repeat(row0[te], TM_MOE)
    occupied = r < jnp.repeat(cnt[te], TM_MOE)
    asg = jnp.clip(jnp.repeat(first_asg[te], TM_MOE) + r, 0, 2 * n - 1)
    spare = 2 * n + p_idx % TM_MOE
    dst_rows = jnp.where(occupied, asg_row[asg], spare)
    src_rows = jnp.where(occupied, dst_rows % n, dst_rows - 2 * n)
    dst_rows = jnp.concatenate([spare[:TM_MOE], dst_rows])
    src_rows = jnp.concatenate([src_rows, jnp.zeros((2 * TM_MOE,), jnp.int32)])
    return (tile_expert, tile_valid, tile_first, src_rows.reshape(n_tiles + 2, 1, TM_MOE),
            dst_rows.reshape(n_tiles + 1, 1, TM_MOE))


def _ple_tail(rows, x1, f, p_ref, g_ref, b_ref, wg_ref, bg_ref, wp_ref, o_ref):
    x2 = _layer_norm(DN_ALPHA * x1 + f, g_ref[...], b_ref[...])
    gate = jax.nn.sigmoid(
        jnp.dot(x2.astype(BF16), wg_ref[...], preferred_element_type=F32) + bg_ref[...])
    emb = jnp.dot(p_ref[rows, :].astype(BF16), wp_ref[...], preferred_element_type=F32)
    o_ref[rows, :] = x2 + gate * emb


def _ple_moe_body(x1_ref, ylo_ref, yhi_ref, route_ref, *rest):
    for i, rows in enumerate(_sub_rows(x1_ref)):
        route = route_ref[rows, :]
        f = (route[:, 2:3] * _load_row_tiles(ylo_ref, i * SUB, SUB)
             + route[:, 3:4] * _load_row_tiles(yhi_ref, i * SUB, SUB))
        _ple_tail(rows, x1_ref[rows, :], f, *rest)


def _ple_call(x1, y, p, g, b, wg, bg, wp, route):
    n = x1.shape[0]
    hi_spec = pl.BlockSpec((TM * ROW_PITCH, LANES), lambda i: (i + n // TM, 0))
    return pl.pallas_call(
        _ple_moe_body, grid=(n // TM,),
        in_specs=[_row_spec(TM, D_MODEL), _row_spec(TM * ROW_PITCH, LANES), hi_spec,
                  _row_spec(TM, LANES),
                  _row_spec(TM, D_PLE), _const_spec(g.shape), _const_spec(b.shape),
                  _const_spec(wg.shape), _const_spec(bg.shape), _const_spec(wp.shape)],
        out_specs=_row_spec(TM, D_MODEL),
        out_shape=jax.ShapeDtypeStruct((n, D_MODEL), F32),
        compiler_params=_params("parallel"), name="ln_ple",
    )(x1, y, y, route, p, g, b, wg, bg, wp)


def _rope_tables(positions):
    inv_freq = 1.0 / (ROPE_THETA ** (jnp.arange(0, QK_ROPE, 2, dtype=F32) / QK_ROPE))
    ang = positions.astype(F32).reshape(-1, 1) * inv_freq
    cos, sin = jnp.cos(ang), jnp.sin(ang)
    n = cos.shape[0]
    ones = jnp.ones((n, ROPE_LANE0), F32)
    zeros = jnp.zeros((n, ROPE_LANE0), F32)
    tail1 = jnp.ones((n, LANES - ROPE_LANE0 - QK_ROPE), F32)
    tail0 = jnp.zeros((n, LANES - ROPE_LANE0 - QK_ROPE), F32)
    rc = jnp.concatenate([ones, cos, cos, tail1], axis=1)
    rs = jnp.concatenate([zeros, -sin, sin, tail0], axis=1)
    return rc, rs


def _row(v):
    return v.reshape(1, -1).astype(F32)


def kernel(x, p, positions, w_in, sg_v_g, sg_v_b, sg_w_s, sg_b_s, q_norm_g, kv_norm_g, w_uq, w_ukv, out_g, w_o, ln1_g, ln1_b, ffn_w1, ffn_w3, ffn_w2, moe_w_r, moe_w1, moe_w3, moe_w2, ln2_g, ln2_b, ple_w_g, ple_b_g, ple_w_p):
    bsz, seq, _ = x.shape
    n = bsz * seq
    xf = x.reshape(n, D_MODEL)
    rc, rs = _rope_tables(positions)

    for i in range(DEPTH):
        win = jnp.concatenate(
            [w_in[i, :, :C_KR0], jnp.zeros((D_MODEL, ROPE_LANE0), F32), w_in[i, :, C_KR0:],
             jnp.zeros((D_MODEL, LANES - ROPE_LANE0 - QK_ROPE), F32)], axis=1).astype(BF16)
        wuq = jnp.pad(w_uq[i].reshape(Q_LORA, N_HEADS, QK_NOPE + QK_ROPE),
                      ((0, 0), (0, 0), (0, HEAD_PAD - QK_NOPE - QK_ROPE)))
        wuq = wuq.reshape(Q_LORA, D_QK_PAD).astype(BF16)
        wukv = w_ukv[i].reshape(KV_LORA, N_HEADS, QK_NOPE + V_DIM)
        wuk = jnp.pad(wukv[:, :, :QK_NOPE], ((0, 0), (0, 0), (0, HEAD_PAD - QK_NOPE)))
        wuk = wuk.reshape(KV_LORA, D_QK_PAD).astype(BF16)
        wuv = jnp.pad(wukv[:, :, QK_NOPE:], ((0, 0), (0, 0), (0, HEAD_PAD - V_DIM)))
        wuv = wuv.reshape(KV_LORA, D_QK_PAD).astype(BF16)
        bexp = jnp.repeat(sg_b_s[i].T, SG_DIM, axis=1)
        a, q, k, v = _proj_call(xf, win, _row(sg_v_g[i]), _row(sg_v_b[i]), sg_w_s[i], bexp,
                                _row(q_norm_g[i]), _row(kv_norm_g[i]), wuq, wuk, wuv, rc, rs,
                                _row(out_g[i, :D_SG]))
        m = _attn_call(q.reshape(bsz, seq, D_QK_PAD), k.reshape(bsz, seq, D_QK_PAD),
                       v.reshape(bsz, seq, D_QK_PAD)).reshape(n, D_ATT)

        j = i // 2
        mix_args = (a, m, xf, _row(out_g[i, D_SG:]), w_o[i].astype(BF16), _row(ln1_g[i]),
                    _row(ln1_b[i]))
        ple_args = (p[i].reshape(n, D_PLE), _row(ln2_g[i]), _row(ln2_b[i]),
                    ple_w_g[i].astype(BF16), _row(ple_b_g[i]), ple_w_p[i].astype(BF16))
        if i % 2 == 0:
            xf = _dense_tail_call(*mix_args, ffn_w1[j].astype(BF16), ffn_w3[j].astype(BF16),
                                  ffn_w2[j].astype(BF16), *ple_args)
        else:
            wr = jnp.pad(moe_w_r[j], ((0, 0), (0, LANES - N_EXPERTS)))
            x1, x1t, route, counts = _oproj_call(*mix_args, wr)
            f = _moe_call(j, *_route_plan(route, counts), x1t, moe_w1, moe_w3, moe_w2)
            xf = _ple_call(x1, f, *ple_args, route)

    return xf.reshape(bsz, seq, D_MODEL)
```

```python
import functools
import math

import jax
import jax.numpy as jnp
from jax import lax
from jax.experimental import pallas as pl
from jax.experimental.pallas import tpu as pltpu

F32 = jnp.float32
BF16 = jnp.bfloat16

D_MODEL = 1024
DEPTH = 4
D_PLE = 256
N_SG = 8
SG_DIM = 64
D_SG = N_SG * SG_DIM
CHUNK = 128
N_HEADS = 8
QK_NOPE = 64
QK_ROPE = 32
V_DIM = 64
Q_LORA = 256
KV_LORA = 128
D_ATT = N_HEADS * V_DIM
ROPE_THETA = 10000.0
D_FF = 2816
N_EXPERTS = 8
D_FF_EXPERT = 3584
DN_ALPHA = (2.0 * DEPTH) ** 0.25
EPS = 1e-6
ATT_SCALE = (QK_NOPE + QK_ROPE) ** -0.5 * math.log2(math.e)

LANES = 128
SUBLANES = 8
HEAD_PAD = LANES
D_QK_PAD = N_HEADS * HEAD_PAD
C_Q0 = 2 * D_SG
C_KV0 = C_Q0 + Q_LORA
C_KR0 = C_KV0 + KV_LORA
D_IN_PAD = C_KR0 + LANES
ROPE_LANE0 = QK_NOPE
ROPE_HALF = QK_ROPE // 2

TM = 512
TM_PROJ = 1024
SUB = 256
TQ = 512
TF_DENSE = 256
ROW_PITCH = SUBLANES + 1
TF_MOE = 256
TM_MOE = 512
VMEM_LIMIT = 56 * 1024 * 1024


def _params(*sem):
    return pltpu.CompilerParams(dimension_semantics=sem, vmem_limit_bytes=VMEM_LIMIT)


def _const_spec(shape):
    nd = len(shape)
    return pl.BlockSpec(shape, lambda *_: (0,) * nd, pipeline_mode=pl.Buffered(1))


def _row_spec(tm, width):
    return pl.BlockSpec((tm, width), lambda i: (i, 0))


def _layer_norm(x, g, b):
    mu = jnp.mean(x, axis=-1, keepdims=True)
    xc = x - mu
    var = jnp.mean(xc * xc, axis=-1, keepdims=True)
    return xc * lax.rsqrt(var + EPS) * g + b


def _rms_norm(x, g):
    return x * lax.rsqrt(jnp.mean(x * x, axis=-1, keepdims=True) + EPS) * g


def _gelu(x):
    return 0.5 * x * (1.0 + lax.erf(x * (1.0 / math.sqrt(2.0))))


def _silu(x):
    return x * jax.nn.sigmoid(x)


def _rope_tile(blk, rc, rs, up):
    nxt = pltpu.roll(blk, LANES - ROPE_HALF, 1)
    prv = pltpu.roll(blk, ROPE_HALF, 1)
    return blk * rc + jnp.where(up, nxt, prv) * rs


def _proj_body(x_ref, win_ref, vg_ref, vb_ref, ws_ref, bexp_ref, qg_ref, kvg_ref,
               wuq_ref, wuk_ref, wuv_ref, rc_ref, rs_ref, og_ref,
               a_ref, q_ref, k_ref, v_ref):
    tm = x_ref.shape[0]
    t_idx = lax.broadcasted_iota(jnp.int32, (CHUNK, CHUNK), 0)
    s_idx = lax.broadcasted_iota(jnp.int32, (CHUNK, CHUNK), 1)
    causal = s_idx <= t_idx
    first_group = s_idx < SG_DIM
    wpair = [
        jnp.concatenate([jnp.where(causal, ws_ref[2 * j], 0.0),
                         jnp.where(causal, ws_ref[2 * j + 1], 0.0)], axis=0).astype(BF16)
        for j in range(N_SG // 2)
    ]
    lane = lax.broadcasted_iota(jnp.int32, (SUB, LANES), 1)
    up = lane < ROPE_LANE0 + ROPE_HALF
    value_lanes = lane < V_DIM

    def in_proj(rows):
        return jnp.dot(x_ref[rows, :].astype(BF16), win_ref[...], preferred_element_type=F32)

    def rest(rows, h):
        uv = _gelu(h[:, :C_Q0])
        u = uv[:, :D_SG]
        v = _layer_norm(uv[:, D_SG:], vg_ref[...], vb_ref[...]).astype(BF16)
        mixed = []
        for c in range(SUB // CHUNK):
            blks = []
            for j in range(N_SG // 2):
                vblk = v[c * CHUNK:(c + 1) * CHUNK, j * LANES:(j + 1) * LANES]
                mm = jnp.dot(wpair[j], vblk, preferred_element_type=F32)
                blks.append(jnp.where(first_group, mm[:CHUNK], mm[CHUNK:]))
            mixed.append(jnp.concatenate(blks, axis=1) + bexp_ref[...])
        a_ref[rows, :] = _rms_norm(u * jnp.concatenate(mixed, axis=0), og_ref[...]).astype(BF16)

        rc = rc_ref[rows, :]
        rs = rs_ref[rows, :]
        cq = _rms_norm(h[:, C_Q0:C_KV0], qg_ref[...]).astype(BF16)
        q = jnp.dot(cq, wuq_ref[...], preferred_element_type=F32)
        for hd in range(N_HEADS):
            sl = slice(hd * HEAD_PAD, (hd + 1) * HEAD_PAD)
            q_ref[rows, sl] = (_rope_tile(q[:, sl], rc, rs, up) * ATT_SCALE).astype(BF16)
        ckv = _rms_norm(h[:, C_KV0:C_KR0], kvg_ref[...]).astype(BF16)
        kn = jnp.dot(ckv, wuk_ref[...], preferred_element_type=F32)
        kr = _rope_tile(h[:, C_KR0:], rc, rs, up)
        for hd in range(N_HEADS):
            sl = slice(hd * HEAD_PAD, (hd + 1) * HEAD_PAD)
            k_ref[rows, sl] = (kn[:, sl] + kr).astype(BF16)
        vv = jnp.dot(ckv, wuv_ref[...], preferred_element_type=F32)
        for hd in range(N_HEADS):
            sl = slice(hd * HEAD_PAD, (hd + 1) * HEAD_PAD)
            v_ref[rows, sl] = jnp.where(value_lanes, vv[:, sl], 1.0).astype(BF16)

    subs = [pl.ds(i * SUB, SUB) for i in range(tm // SUB)]
    h_next = in_proj(subs[0])
    for i, rows in enumerate(subs):
        h = h_next
        if i + 1 < len(subs):
            h_next = in_proj(subs[i + 1])
        rest(rows, h)


def _proj_call(x, win, vg, vb, ws, bexp, qg, kvg, wuq, wuk, wuv, rc, rs, og):
    n = x.shape[0]
    return pl.pallas_call(
        _proj_body,
        grid=(n // TM_PROJ,),
        in_specs=[
            _row_spec(TM_PROJ, D_MODEL),
            _const_spec(win.shape), _const_spec(vg.shape), _const_spec(vb.shape),
            _const_spec(ws.shape), _const_spec(bexp.shape), _const_spec(qg.shape),
            _const_spec(kvg.shape), _const_spec(wuq.shape), _const_spec(wuk.shape),
            _const_spec(wuv.shape),
            _row_spec(TM_PROJ, LANES), _row_spec(TM_PROJ, LANES),
            _const_spec(og.shape),
        ],
        out_specs=[_row_spec(TM_PROJ, D_SG)] + [_row_spec(TM_PROJ, D_QK_PAD)] * 3,
        out_shape=[jax.ShapeDtypeStruct((n, D_SG), BF16)]
        + [jax.ShapeDtypeStruct((n, D_QK_PAD), BF16)] * 3,
        compiler_params=_params("parallel"),
        name="proj",
    )(x, win, vg, vb, ws, bexp, qg, kvg, wuq, wuk, wuv, rc, rs, og)


def _attn_body(q_ref, k_ref, v_ref, o_ref, m_sc, acc_sc):
    tq = q_ref.shape[1]
    half = tq // 2
    qi = pl.program_id(1)
    ahead = 2

    def head(hh):
        return slice(hh * HEAD_PAD, (hh + 1) * HEAD_PAD)

    def qk(rows, keys, hh):
        return lax.dot_general(q_ref[0, rows, head(hh)], k_ref[0, keys, head(hh)],
                               (((1,), (1,)), ((), ())), preferred_element_type=F32)

    def probs(s, m):
        return jnp.exp2(s - jnp.concatenate([m] * (s.shape[1] // LANES), axis=1)).astype(BF16)

    def diagonal_block(off):
        parts = ((pl.ds(0, half), pl.ds(off, half), 0), (pl.ds(half, half), pl.ds(off, tq), half))
        masks = []
        for _, keys, shift in parts:
            row = lax.broadcasted_iota(jnp.int32, (half, keys.size), 0)
            col = lax.broadcasted_iota(jnp.int32, (half, keys.size), 1)
            masks.append(col <= row + shift)
        queue = [[qk(rows, keys, hh) for rows, keys, _ in parts] for hh in range(ahead)]
        for hh in range(N_HEADS):
            scores = queue.pop(0)
            if hh + ahead < N_HEADS:
                queue.append([qk(rows, keys, hh + ahead) for rows, keys, _ in parts])
            for (rows, keys, _), mask, s in zip(parts, masks, scores):
                s = jnp.where(mask, s, -jnp.inf)
                m = jnp.broadcast_to(jnp.max(s, axis=1, keepdims=True), (half, LANES))
                acc_sc[hh, rows, :] = jnp.dot(probs(s, m), v_ref[0, keys, head(hh)],
                                              preferred_element_type=F32)
                m_sc[hh, rows, :] = m

    def full_block(ki, carry):
        keys = pl.ds(pl.multiple_of(ki * tq, tq), tq)
        rows = pl.ds(0, tq)
        queue = [qk(rows, keys, hh) for hh in range(ahead)]
        for hh in range(N_HEADS):
            s = queue.pop(0)
            if hh + ahead < N_HEADS:
                queue.append(qk(rows, keys, hh + ahead))
            m_prev = m_sc[hh]
            m_next = jnp.maximum(m_prev, jnp.max(s, axis=1, keepdims=True))
            acc_sc[hh] = jnp.exp2(m_prev - m_next) * acc_sc[hh] + jnp.dot(
                probs(s, m_next), v_ref[0, keys, head(hh)], preferred_element_type=F32)
            m_sc[hh] = m_next
        return carry

    diagonal_block(pl.multiple_of(qi * tq, tq))
    lax.fori_loop(0, qi, full_block, 0)

    value_lanes = lax.broadcasted_iota(jnp.int32, (tq, LANES), 1) < V_DIM
    outs = []
    for hh in range(N_HEADS):
        acc = acc_sc[hh]
        row_sum = jnp.where(value_lanes, pltpu.roll(acc, V_DIM, 1), 1.0)
        outs.append(acc / row_sum)
    for pair in range(N_HEADS // 2):
        o_ref[0, :, pair * LANES:(pair + 1) * LANES] = jnp.where(
            value_lanes, outs[2 * pair], pltpu.roll(outs[2 * pair + 1], V_DIM, 1))


def _attn_call(q, k, v):
    b, s, _ = q.shape
    kv_spec = pl.BlockSpec((1, s, D_QK_PAD), lambda bi, qi: (bi, 0, 0))
    return pl.pallas_call(
        _attn_body,
        grid=(b, s // TQ),
        in_specs=[pl.BlockSpec((1, TQ, D_QK_PAD), lambda bi, qi: (bi, qi, 0)), kv_spec, kv_spec],
        out_specs=pl.BlockSpec((1, TQ, D_ATT), lambda bi, qi: (bi, qi, 0)),
        out_shape=jax.ShapeDtypeStruct((b, s, D_ATT), F32),
        scratch_shapes=[pltpu.VMEM((N_HEADS, TQ, LANES), F32)] * 2,
        compiler_params=_params("parallel", "arbitrary"),
        name="attn",
    )(q, k, v)


def _mix_out(rows, a_ref, m_ref, x_ref, ogm_ref, wo_ref, g_ref, b_ref):
    mn = _rms_norm(m_ref[rows, :], ogm_ref[...]).astype(BF16)
    y = jnp.dot(a_ref[rows, :], wo_ref[:D_SG, :], preferred_element_type=F32)
    y = y + jnp.dot(mn, wo_ref[D_SG:, :], preferred_element_type=F32)
    return _layer_norm(DN_ALPHA * x_ref[rows, :] + y, g_ref[...], b_ref[...])


def _sub_rows(ref):
    return [pl.ds(i * SUB, SUB) for i in range(ref.shape[0] // SUB)]


def _store_row_tiles(ref, t0, val):
    nt = val.shape[0]
    for j in range(D_MODEL // LANES):
        ref[pl.ds(ROW_PITCH * t0 + j, nt, stride=ROW_PITCH), :] = val[:, j * LANES:(j + 1) * LANES]
    ref[pl.ds(ROW_PITCH * t0 + SUBLANES, nt, stride=ROW_PITCH), :] = jnp.zeros((nt, LANES), F32)


def _load_row_tiles(ref, t0, nt):
    return jnp.concatenate([ref[pl.ds(ROW_PITCH * t0 + j, nt, stride=ROW_PITCH), :]
                            for j in range(D_MODEL // LANES)], axis=1)


def _oproj_route_body(a_ref, m_ref, x_ref, ogm_ref, wo_ref, g_ref, b_ref, wrh_ref, wrl_ref,
                      x1_ref, x1t_ref, route_ref, cnt_ref, cnt_sc):
    @pl.when(pl.program_id(0) == 0)
    def _():
        cnt_sc[...] = jnp.zeros(cnt_sc.shape, F32)

    lane = lax.broadcasted_iota(jnp.int32, (SUB, LANES), 1)
    lane_f = lane.astype(F32)
    for i, rows in enumerate(_sub_rows(x_ref)):
        x1 = _mix_out(rows, a_ref, m_ref, x_ref, ogm_ref, wo_ref, g_ref, b_ref)
        x1_ref[rows, :] = x1
        _store_row_tiles(x1t_ref, i * SUB, x1)
        x_hi = x1.astype(BF16)
        x_lo = (x1 - x_hi.astype(F32)).astype(BF16)
        logits = (jnp.dot(x_hi, wrh_ref[...], preferred_element_type=F32)
                  + jnp.dot(x_lo, wrh_ref[...], preferred_element_type=F32)
                  + jnp.dot(x_hi, wrl_ref[...], preferred_element_type=F32))
        lg = jnp.where(lane < N_EXPERTS, logits, -jnp.inf)
        m1 = jnp.max(lg, axis=1, keepdims=True)
        i1 = jnp.min(jnp.where(lg == m1, lane_f, float(LANES)), axis=1, keepdims=True)
        lg2 = jnp.where(lane_f == i1, -jnp.inf, lg)
        m2 = jnp.max(lg2, axis=1, keepdims=True)
        i2 = jnp.min(jnp.where(lg2 == m2, lane_f, float(LANES)), axis=1, keepdims=True)
        e = jnp.exp(m2 - m1)
        w1 = 1.0 / (1.0 + e)
        w2 = e / (1.0 + e)
        first_lo = i1 < i2
        e_lo = jnp.where(first_lo, i1, i2)
        e_hi = jnp.where(first_lo, i2, i1)
        w_lo = jnp.where(first_lo, w1, w2)
        w_hi = jnp.where(first_lo, w2, w1)

        onehot = jnp.where((lane_f == e_lo) | (lane_f == e_hi), 1.0, 0.0).astype(BF16)
        cnt_sc[...] += jnp.dot(jnp.ones((SUBLANES, SUB), BF16), onehot,
                               preferred_element_type=F32)
        route = jnp.zeros((SUB, LANES), F32)
        for idx, colv in enumerate((e_lo, e_hi, w_lo, w_hi)):
            route = jnp.where(lane == idx, colv, route)
        route_ref[rows, :] = route
    cnt_ref[...] = cnt_sc[...]


def _oproj_call(a, m, x, ogm, wo, g, b, wr):
    n = x.shape[0]
    in_specs = [_row_spec(TM, D_SG), _row_spec(TM, D_ATT), _row_spec(TM, D_MODEL),
                _const_spec(ogm.shape), _const_spec(wo.shape), _const_spec(g.shape),
                _const_spec(b.shape)]
    wr_hi = wr.astype(BF16)
    wr_lo = (wr - wr_hi.astype(F32)).astype(BF16)
    return pl.pallas_call(
        _oproj_route_body, grid=(n // TM,),
        in_specs=in_specs + [_const_spec(wr_hi.shape), _const_spec(wr_lo.shape)],
        out_specs=[_row_spec(TM, D_MODEL), _row_spec(TM * ROW_PITCH, LANES), _row_spec(TM, LANES),
                   pl.BlockSpec((SUBLANES, LANES), lambda i: (0, 0))],
        out_shape=[jax.ShapeDtypeStruct((n, D_MODEL), F32),
                   jax.ShapeDtypeStruct((n * ROW_PITCH, LANES), F32),
                   jax.ShapeDtypeStruct((n, LANES), F32),
                   jax.ShapeDtypeStruct((SUBLANES, LANES), F32)],
        scratch_shapes=[pltpu.VMEM((SUBLANES, LANES), F32)],
        compiler_params=_params("arbitrary"), name="oproj_route",
    )(a, m, x, ogm, wo, g, b, wr_hi, wr_lo)


def _dense_tail_body(a_ref, m_ref, x_ref, ogm_ref, wo_ref, g1_ref, b1_ref, w1_ref, w3_ref, w2_ref,
                     p_ref, g2_ref, b2_ref, wg_ref, bg_ref, wp_ref, o_ref, x1_sc):
    for rows in _sub_rows(x_ref):
        x1_sc[rows, :] = _mix_out(rows, a_ref, m_ref, x_ref, ogm_ref, wo_ref, g1_ref, b1_ref)
    xb = x1_sc[...].astype(BF16)
    f = None
    for c in range(D_FF // TF_DENSE):
        sl = slice(c * TF_DENSE, (c + 1) * TF_DENSE)
        g = jnp.dot(xb, w1_ref[:, sl], preferred_element_type=F32)
        u = jnp.dot(xb, w3_ref[:, sl], preferred_element_type=F32)
        part = jnp.dot((_silu(g) * u).astype(BF16), w2_ref[sl, :], preferred_element_type=F32)
        f = part if f is None else f + part
    for i, rows in enumerate(_sub_rows(x_ref)):
        _ple_tail(rows, x1_sc[rows, :], f[i * SUB:(i + 1) * SUB, :], p_ref, g2_ref, b2_ref,
                  wg_ref, bg_ref, wp_ref, o_ref)


def _ple_spec(layer, n):
    return pl.BlockSpec((TM, D_PLE), lambda i: (i + layer * (n // TM), 0))


def _dense_tail_call(layer, a, m, x, ogm, wo, g1, b1, w1, w3, w2, p, g2, b2, wg, bg, wp):
    n = x.shape[0]
    consts = (ogm, wo, g1, b1, w1, w3, w2)
    tail_consts = (g2, b2, wg, bg, wp)
    return pl.pallas_call(
        _dense_tail_body, grid=(n // TM,),
        in_specs=[_row_spec(TM, D_SG), _row_spec(TM, D_ATT), _row_spec(TM, D_MODEL)]
        + [_const_spec(c.shape) for c in consts] + [_ple_spec(layer, n)]
        + [_const_spec(c.shape) for c in tail_consts],
        out_specs=_row_spec(TM, D_MODEL),
        out_shape=jax.ShapeDtypeStruct((n, D_MODEL), F32),
        scratch_shapes=[pltpu.VMEM((TM, D_MODEL), F32)],
        compiler_params=_params("parallel"), name="dense_tail",
    )(a, m, x, *consts, p, *tail_consts)


def _moe_body(layer, te_ref, tv_ref, tf_ref, src0_ref, src1_ref, dst_ref, x_hbm, w1_hbm, w3_hbm,
              w2_hbm, y_hbm, xbuf, xb_sc, ybuf, wb1, wb3, wb2, st1, st3, st2, gsem, ssem, wsem):
    t = pl.program_id(0)
    tm = xb_sc.shape[0]
    n_chunks = wb1.shape[0]
    rows_per_chunk = -(-tm // (n_chunks - 2))
    expert = te_ref[t]
    n_tile_rows = ROW_PITCH * tm

    def dma_thread(r):
        return r % 2 if isinstance(r, int) else 0

    def start_gather(src_ref, lo, hi):
        for r in range(lo, hi) if isinstance(lo, int) else (lo,):
            pltpu.make_async_copy(x_hbm.at[pl.ds(ROW_PITCH * src_ref[0, 0, r], ROW_PITCH), :],
                                  xbuf.at[pl.ds(ROW_PITCH * r, ROW_PITCH), :],
                                  gsem).start(priority=dma_thread(r))

    def start_scatter(lo, hi):
        for r in range(lo, hi) if isinstance(lo, int) else (lo,):
            pltpu.make_async_copy(ybuf.at[pl.ds(ROW_PITCH * r, ROW_PITCH), :],
                                  y_hbm.at[pl.ds(ROW_PITCH * dst_ref[0, 0, r], ROW_PITCH), :],
                                  ssem).start(priority=dma_thread(r))

    def start_rolled(start_rows):
        def body(r, carry):
            start_rows(r, r + 1)
            return carry
        lax.fori_loop(0, tm, body, 0)

    def wait_gather():
        pltpu.make_async_copy(x_hbm.at[pl.ds(0, n_tile_rows), :], xbuf.at[pl.ds(0, n_tile_rows), :],
                              gsem).wait()

    def wait_scatter():
        pltpu.make_async_copy(ybuf.at[pl.ds(0, n_tile_rows), :], y_hbm.at[pl.ds(0, n_tile_rows), :],
                              ssem).wait()

    def weight_copies(c, slot):
        sl = pl.ds(c * TF_MOE, TF_MOE)
        sem = wsem.at[slot]
        return (pltpu.make_async_copy(w1_hbm.at[layer, expert, :, sl], st1.at[slot], sem),
                pltpu.make_async_copy(w3_hbm.at[layer, expert, :, sl], st3.at[slot], sem),
                pltpu.make_async_copy(w2_hbm.at[layer, expert, sl, :], st2.at[slot], sem))

    def row_copies_after(c, part):
        pin = part[0:SUBLANES, 0:LANES]
        xbuf[n_tile_rows:n_tile_rows + SUBLANES, :] = pin
        ybuf[n_tile_rows:n_tile_rows + SUBLANES, :] = pin
        lo, hi = min(c * rows_per_chunk, tm), min((c + 1) * rows_per_chunk, tm)
        start_gather(src1_ref, lo, hi)
        start_scatter(lo, hi)

    def swiglu_tile(load_weights):
        xb = xb_sc[...]
        if load_weights:
            for c in range(2):
                for cp in weight_copies(c, c):
                    cp.start()
        acc = None
        for c in range(n_chunks):
            if load_weights:
                slot = c % 2
                for cp in weight_copies(c, slot):
                    cp.wait()
                w1c = st1[slot].astype(BF16)
                w3c = st3[slot].astype(BF16)
                w2c = st2[slot].astype(BF16)
                wb1[c] = w1c
                wb3[c] = w3c
                wb2[c] = w2c
                if c + 2 < n_chunks:
                    for cp in weight_copies(c + 2, slot):
                        cp.start()
            else:
                w1c, w3c, w2c = wb1[c], wb3[c], wb2[c]
            g = jnp.dot(xb, w1c, preferred_element_type=F32)
            u = jnp.dot(xb, w3c, preferred_element_type=F32)
            part = jnp.dot((_silu(g) * u).astype(BF16), w2c, preferred_element_type=F32)
            acc = part if acc is None else acc + part
            row_copies_after(c, part)
        wait_scatter()
        _store_row_tiles(ybuf, 0, acc)

    @pl.when(t == 0)
    def _():
        ybuf[...] = jnp.zeros(ybuf.shape, F32)
        start_rolled(functools.partial(start_gather, src0_ref))

    wait_gather()
    xb_sc[...] = _load_row_tiles(xbuf, 0, tm).astype(BF16)
    valid = tv_ref[t] == 1
    first = tf_ref[t] == 1

    @pl.when(valid & first)
    def _():
        swiglu_tile(True)

    @pl.when(valid & jnp.logical_not(first))
    def _():
        swiglu_tile(False)

    @pl.when(jnp.logical_not(valid))
    def _():
        start_rolled(functools.partial(start_gather, src1_ref))
        start_rolled(start_scatter)
        wait_scatter()

    @pl.when(t == pl.num_programs(0) - 1)
    def _():
        wait_gather()


def _moe_call(layer, tile_expert, tile_valid, tile_first, src_rows, dst_rows, x1, w1, w3, w2):
    n = x1.shape[0] // ROW_PITCH
    n_steps = dst_rows.shape[0]
    n_chunks = D_FF_EXPERT // TF_MOE

    def row_block(offset):
        return pl.BlockSpec((1, 1, TM_MOE), lambda t, te, tv, tf: (t + offset, 0, 0),
                            memory_space=pltpu.SMEM)

    hbm = pl.BlockSpec(memory_space=pl.ANY)
    return pl.pallas_call(
        functools.partial(_moe_body, layer),
        grid_spec=pltpu.PrefetchScalarGridSpec(
            num_scalar_prefetch=3,
            grid=(n_steps,),
            in_specs=[row_block(0), row_block(1), row_block(0), hbm, hbm, hbm, hbm],
            out_specs=hbm,
            scratch_shapes=[pltpu.VMEM((TM_MOE * ROW_PITCH + SUBLANES, LANES), F32),
                            pltpu.VMEM((TM_MOE, D_MODEL), BF16),
                            pltpu.VMEM((TM_MOE * ROW_PITCH + SUBLANES, LANES), F32),
                            pltpu.VMEM((n_chunks, D_MODEL, TF_MOE), BF16),
                            pltpu.VMEM((n_chunks, D_MODEL, TF_MOE), BF16),
                            pltpu.VMEM((n_chunks, TF_MOE, D_MODEL), BF16),
                            pltpu.VMEM((2, D_MODEL, TF_MOE), F32),
                            pltpu.VMEM((2, D_MODEL, TF_MOE), F32),
                            pltpu.VMEM((2, TF_MOE, D_MODEL), F32),
                            pltpu.SemaphoreType.DMA, pltpu.SemaphoreType.DMA,
                            pltpu.SemaphoreType.DMA((2,))],
        ),
        out_shape=jax.ShapeDtypeStruct(((2 * n + TM_MOE) * ROW_PITCH, LANES), F32),
        compiler_params=_params("arbitrary"), name="moe_experts",
    )(tile_expert, tile_valid, tile_first, src_rows, src_rows, dst_rows, x1, w1, w3, w2)


def _route_plan(route, counts):
    n = route.shape[0]
    n_tiles = 2 * n // TM_MOE + N_EXPERTS - 1
    cnt = counts[0, :N_EXPERTS].astype(jnp.int32)
    tiles = (cnt + TM_MOE - 1) // TM_MOE
    tile_end = jnp.cumsum(tiles)
    row0 = (tile_end - tiles) * TM_MOE
    first_asg = jnp.cumsum(cnt) - cnt
    t_idx = jnp.arange(n_tiles + 1)
    tile_valid = (t_idx < tile_end[-1]).astype(jnp.int32)
    tile_expert = jnp.minimum(jnp.sum(t_idx[:, None] >= tile_end[None, :], axis=1),
                              N_EXPERTS - 1).astype(jnp.int32)
    tile_first = jnp.concatenate([jnp.ones((1,), jnp.int32),
                                  (tile_expert[1:] != tile_expert[:-1]).astype(jnp.int32)])
    e_pair = route[:, 0:2].astype(jnp.int32)
    tok = jnp.arange(n, dtype=jnp.int32)
    keys = (e_pair * n + tok[:, None]) * 2 + jnp.arange(2, dtype=jnp.int32)[None, :]
    keys = jnp.sort(keys.reshape(-1))
    asg_row = (keys % 2) * n + (keys // 2) % n
    p_idx = jnp.arange(n_tiles * TM_MOE, dtype=jnp.int32)
    te = tile_expert[:n_tiles]
    r = p_idx - jnp.repeat(row0[te], TM_MOE)
    occupied = r < jnp.repeat(cnt[te], TM_MOE)
    asg = jnp.clip(jnp.repeat(first_asg[te], TM_MOE) + r, 0, 2 * n - 1)
    spare = 2 * n + p_idx % TM_MOE
    dst_rows = jnp.where(occupied, asg_row[asg], spare)
    src_rows = jnp.where(occupied, dst_rows % n, dst_rows - 2 * n)
    dst_rows = jnp.concatenate([spare[:TM_MOE], dst_rows])
    src_rows = jnp.concatenate([src_rows, jnp.zeros((2 * TM_MOE,), jnp.int32)])
    return (tile_expert, tile_valid, tile_first, src_rows.reshape(n_tiles + 2, 1, TM_MOE),
            dst_rows.reshape(n_tiles + 1, 1, TM_MOE))


def _ple_tail(rows, x1, f, p_ref, g_ref, b_ref, wg_ref, bg_ref, wp_ref, o_ref):
    x2 = _layer_norm(DN_ALPHA * x1 + f, g_ref[...], b_ref[...])
    gate = jax.nn.sigmoid(
        jnp.dot(x2.astype(BF16), wg_ref[...], preferred_element_type=F32) + bg_ref[...])
    emb = jnp.dot(p_ref[rows, :].astype(BF16), wp_ref[...], preferred_element_type=F32)
    o_ref[rows, :] = x2 + gate * emb


def _ple_moe_body(x1_ref, ylo_ref, yhi_ref, route_ref, *rest):
    for i, rows in enumerate(_sub_rows(x1_ref)):
        route = route_ref[rows, :]
        f = (route[:, 2:3] * _load_row_tiles(ylo_ref, i * SUB, SUB)
             + route[:, 3:4] * _load_row_tiles(yhi_ref, i * SUB, SUB))
        _ple_tail(rows, x1_ref[rows, :], f, *rest)


def _ple_call(layer, x1, y, p, g, b, wg, bg, wp, route):
    n = x1.shape[0]
    hi_spec = pl.BlockSpec((TM * ROW_PITCH, LANES), lambda i: (i + n // TM, 0))
    return pl.pallas_call(
        _ple_moe_body, grid=(n // TM,),
        in_specs=[_row_spec(TM, D_MODEL), _row_spec(TM * ROW_PITCH, LANES), hi_spec,
                  _row_spec(TM, LANES),
                  _ple_spec(layer, n), _const_spec(g.shape), _const_spec(b.shape),
                  _const_spec(wg.shape), _const_spec(bg.shape), _const_spec(wp.shape)],
        out_specs=_row_spec(TM, D_MODEL),
        out_shape=jax.ShapeDtypeStruct((n, D_MODEL), F32),
        compiler_params=_params("parallel"), name="ln_ple",
    )(x1, y, y, route, p, g, b, wg, bg, wp)


def _rope_tables(positions):
    inv_freq = 1.0 / (ROPE_THETA ** (jnp.arange(0, QK_ROPE, 2, dtype=F32) / QK_ROPE))
    lane = jnp.arange(LANES)
    rotary = (lane >= ROPE_LANE0) & (lane < ROPE_LANE0 + QK_ROPE)
    freq = jnp.where(rotary, inv_freq[(lane - ROPE_LANE0) % ROPE_HALF], 0.0)
    sign = jnp.where(lane < ROPE_LANE0 + ROPE_HALF, -1.0, 1.0).astype(F32)
    ang = positions.astype(F32).reshape(-1, 1) * freq[None, :]
    return jnp.cos(ang), jnp.sin(ang) * sign[None, :]


def _row(v):
    return v.reshape(1, -1).astype(F32)


def kernel(x, p, positions, w_in, sg_v_g, sg_v_b, sg_w_s, sg_b_s, q_norm_g, kv_norm_g, w_uq, w_ukv, out_g, w_o, ln1_g, ln1_b, ffn_w1, ffn_w3, ffn_w2, moe_w_r, moe_w1, moe_w3, moe_w2, ln2_g, ln2_b, ple_w_g, ple_b_g, ple_w_p):
    bsz, seq, _ = x.shape
    n = bsz * seq
    xf = x.reshape(n, D_MODEL)
    rc, rs = _rope_tables(positions)

    for i in range(DEPTH):
        win = jnp.concatenate(
            [w_in[i, :, :C_KR0], jnp.zeros((D_MODEL, ROPE_LANE0), F32), w_in[i, :, C_KR0:],
             jnp.zeros((D_MODEL, LANES - ROPE_LANE0 - QK_ROPE), F32)], axis=1).astype(BF16)
        wuq = jnp.pad(w_uq[i].reshape(Q_LORA, N_HEADS, QK_NOPE + QK_ROPE),
                      ((0, 0), (0, 0), (0, HEAD_PAD - QK_NOPE - QK_ROPE)))
        wuq = wuq.reshape(Q_LORA, D_QK_PAD).astype(BF16)
        wukv = w_ukv[i].reshape(KV_LORA, N_HEADS, QK_NOPE + V_DIM)
        wuk = jnp.pad(wukv[:, :, :QK_NOPE], ((0, 0), (0, 0), (0, HEAD_PAD - QK_NOPE)))
        wuk = wuk.reshape(KV_LORA, D_QK_PAD).astype(BF16)
        wuv = jnp.pad(wukv[:, :, QK_NOPE:], ((0, 0), (0, 0), (0, HEAD_PAD - V_DIM)))
        wuv = wuv.reshape(KV_LORA, D_QK_PAD).astype(BF16)
        bexp = jnp.repeat(sg_b_s[i].T, SG_DIM, axis=1)
        a, q, k, v = _proj_call(xf, win, _row(sg_v_g[i]), _row(sg_v_b[i]), sg_w_s[i], bexp,
                                _row(q_norm_g[i]), _row(kv_norm_g[i]), wuq, wuk, wuv, rc, rs,
                                _row(out_g[i, :D_SG]))
        m = _attn_call(q.reshape(bsz, seq, D_QK_PAD), k.reshape(bsz, seq, D_QK_PAD),
                       v.reshape(bsz, seq, D_QK_PAD)).reshape(n, D_ATT)

        j = i // 2
        mix_args = (a, m, xf, _row(out_g[i, D_SG:]), w_o[i].astype(BF16), _row(ln1_g[i]),
                    _row(ln1_b[i]))
        ple_args = (p.reshape(DEPTH * n, D_PLE), _row(ln2_g[i]), _row(ln2_b[i]),
                    ple_w_g[i].astype(BF16), _row(ple_b_g[i]), ple_w_p[i].astype(BF16))
        if i % 2 == 0:
            xf = _dense_tail_call(i, *mix_args, ffn_w1[j].astype(BF16), ffn_w3[j].astype(BF16),
                                  ffn_w2[j].astype(BF16), *ple_args)
        else:
            wr = jnp.pad(moe_w_r[j], ((0, 0), (0, LANES - N_EXPERTS)))
            x1, x1t, route, counts = _oproj_call(*mix_args, wr)
            f = _moe_call(j, *_route_plan(route, counts), x1t, moe_w1, moe_w3, moe_w2)
            xf = _ple_call(i, x1, f, *ple_args, route)

    return xf.reshape(bsz, seq, D_MODEL)
```

```python
import functools
import math

import jax
import jax.numpy as jnp
from jax import lax
from jax.experimental import pallas as pl
from jax.experimental.pallas import tpu as pltpu

F32 = jnp.float32
BF16 = jnp.bfloat16

D_MODEL = 1024
DEPTH = 4
D_PLE = 256
N_SG = 8
SG_DIM = 64
D_SG = N_SG * SG_DIM
CHUNK = 128
N_HEADS = 8
QK_NOPE = 64
QK_ROPE = 32
V_DIM = 64
Q_LORA = 256
KV_LORA = 128
D_ATT = N_HEADS * V_DIM
ROPE_THETA = 10000.0
D_FF = 2816
N_EXPERTS = 8
D_FF_EXPERT = 3584
DN_ALPHA = (2.0 * DEPTH) ** 0.25
EPS = 1e-6
ATT_SCALE = (QK_NOPE + QK_ROPE) ** -0.5 * math.log2(math.e)

LANES = 128
SUBLANES = 8
HEAD_PAD = LANES
D_QK_PAD = N_HEADS * HEAD_PAD
C_Q0 = 2 * D_SG
C_KV0 = C_Q0 + Q_LORA
C_KR0 = C_KV0 + KV_LORA
D_IN_PAD = C_KR0 + LANES
ROPE_LANE0 = QK_NOPE
ROPE_HALF = QK_ROPE // 2

TM = 512
TM_PROJ = 1024
SUB = 256
TQ = 512
TF_DENSE = 256
ROW_PITCH = SUBLANES + 1
TF_MOE = 256
TM_MOE = 512
VMEM_LIMIT = 56 * 1024 * 1024


def _params(*sem):
    return pltpu.CompilerParams(dimension_semantics=sem, vmem_limit_bytes=VMEM_LIMIT)


def _const_spec(shape):
    nd = len(shape)
    return pl.BlockSpec(shape, lambda *_: (0,) * nd, pipeline_mode=pl.Buffered(1))


def _row_spec(tm, width):
    return pl.BlockSpec((tm, width), lambda i: (i, 0))


def _layer_norm(x, g, b):
    mu = jnp.mean(x, axis=-1, keepdims=True)
    xc = x - mu
    var = jnp.mean(xc * xc, axis=-1, keepdims=True)
    return xc * lax.rsqrt(var + EPS) * g + b


def _rms_norm(x, g):
    return x * lax.rsqrt(jnp.mean(x * x, axis=-1, keepdims=True) + EPS) * g


def _gelu(x):
    return 0.5 * x * (1.0 + lax.erf(x * (1.0 / math.sqrt(2.0))))


def _silu(x):
    return x * jax.nn.sigmoid(x)


def _rope_tile(blk, rc, rs, up):
    nxt = pltpu.roll(blk, LANES - ROPE_HALF, 1)
    prv = pltpu.roll(blk, ROPE_HALF, 1)
    return blk * rc + jnp.where(up, nxt, prv) * rs


def _proj_body(x_ref, win_ref, vg_ref, vb_ref, ws_ref, bexp_ref, qg_ref, kvg_ref,
               wuq_ref, wuk_ref, wuv_ref, rc_ref, rs_ref, og_ref,
               a_ref, q_ref, k_ref, v_ref):
    tm = x_ref.shape[0]
    t_idx = lax.broadcasted_iota(jnp.int32, (CHUNK, CHUNK), 0)
    s_idx = lax.broadcasted_iota(jnp.int32, (CHUNK, CHUNK), 1)
    causal = s_idx <= t_idx
    first_group = s_idx < SG_DIM
    wpair = [
        jnp.concatenate([jnp.where(causal, ws_ref[2 * j], 0.0),
                         jnp.where(causal, ws_ref[2 * j + 1], 0.0)], axis=0).astype(BF16)
        for j in range(N_SG // 2)
    ]
    lane = lax.broadcasted_iota(jnp.int32, (SUB, LANES), 1)
    up = lane < ROPE_LANE0 + ROPE_HALF
    value_lanes = lane < V_DIM

    def in_proj(rows):
        return jnp.dot(x_ref[rows, :].astype(BF16), win_ref[...], preferred_element_type=F32)

    def rest(rows, h):
        uv = _gelu(h[:, :C_Q0])
        u = uv[:, :D_SG]
        v = _layer_norm(uv[:, D_SG:], vg_ref[...], vb_ref[...]).astype(BF16)
        mixed = []
        for c in range(SUB // CHUNK):
            blks = []
            for j in range(N_SG // 2):
                vblk = v[c * CHUNK:(c + 1) * CHUNK, j * LANES:(j + 1) * LANES]
                mm = jnp.dot(wpair[j], vblk, preferred_element_type=F32)
                blks.append(jnp.where(first_group, mm[:CHUNK], mm[CHUNK:]))
            mixed.append(jnp.concatenate(blks, axis=1) + bexp_ref[...])
        a_ref[rows, :] = _rms_norm(u * jnp.concatenate(mixed, axis=0), og_ref[...]).astype(BF16)

        rc = rc_ref[rows, :]
        rs = rs_ref[rows, :]
        cq = _rms_norm(h[:, C_Q0:C_KV0], qg_ref[...]).astype(BF16)
        q = jnp.dot(cq, wuq_ref[...], preferred_element_type=F32)
        for hd in range(N_HEADS):
            sl = slice(hd * HEAD_PAD, (hd + 1) * HEAD_PAD)
            q_ref[rows, sl] = (_rope_tile(q[:, sl], rc, rs, up) * ATT_SCALE).astype(BF16)
        ckv = _rms_norm(h[:, C_KV0:C_KR0], kvg_ref[...]).astype(BF16)
        kn = jnp.dot(ckv, wuk_ref[...], preferred_element_type=F32)
        kr = _rope_tile(h[:, C_KR0:], rc, rs, up)
        for hd in range(N_HEADS):
            sl = slice(hd * HEAD_PAD, (hd + 1) * HEAD_PAD)
            k_ref[rows, sl] = (kn[:, sl] + kr).astype(BF16)
        vv = jnp.dot(ckv, wuv_ref[...], preferred_element_type=F32)
        for hd in range(N_HEADS):
            sl = slice(hd * HEAD_PAD, (hd + 1) * HEAD_PAD)
            v_ref[rows, sl] = jnp.where(value_lanes, vv[:, sl], 1.0).astype(BF16)

    subs = [pl.ds(i * SUB, SUB) for i in range(tm // SUB)]
    h_next = in_proj(subs[0])
    for i, rows in enumerate(subs):
        h = h_next
        if i + 1 < len(subs):
            h_next = in_proj(subs[i + 1])
        rest(rows, h)


def _proj_call(x, win, vg, vb, ws, bexp, qg, kvg, wuq, wuk, wuv, rc, rs, og):
    n = x.shape[0]
    return pl.pallas_call(
        _proj_body,
        grid=(n // TM_PROJ,),
        in_specs=[
            _row_spec(TM_PROJ, D_MODEL),
            _const_spec(win.shape), _const_spec(vg.shape), _const_spec(vb.shape),
            _const_spec(ws.shape), _const_spec(bexp.shape), _const_spec(qg.shape),
            _const_spec(kvg.shape), _const_spec(wuq.shape), _const_spec(wuk.shape),
            _const_spec(wuv.shape),
            _row_spec(TM_PROJ, LANES), _row_spec(TM_PROJ, LANES),
            _const_spec(og.shape),
        ],
        out_specs=[_row_spec(TM_PROJ, D_SG)] + [_row_spec(TM_PROJ, D_QK_PAD)] * 3,
        out_shape=[jax.ShapeDtypeStruct((n, D_SG), BF16)]
        + [jax.ShapeDtypeStruct((n, D_QK_PAD), BF16)] * 3,
        compiler_params=_params("parallel"),
        name="proj",
    )(x, win, vg, vb, ws, bexp, qg, kvg, wuq, wuk, wuv, rc, rs, og)


def _attn_body(q_ref, k_ref, v_ref, o_ref, m_sc, acc_sc):
    tq = q_ref.shape[1]
    half = tq // 2
    qi = pl.program_id(1)
    ahead = 2

    def head(hh):
        return slice(hh * HEAD_PAD, (hh + 1) * HEAD_PAD)

    def qk(rows, keys, hh):
        return lax.dot_general(q_ref[0, rows, head(hh)], k_ref[0, keys, head(hh)],
                               (((1,), (1,)), ((), ())), preferred_element_type=F32)

    def probs(s, m):
        return jnp.exp2(s - jnp.concatenate([m] * (s.shape[1] // LANES), axis=1)).astype(BF16)

    def diagonal_block(off):
        parts = ((pl.ds(0, half), pl.ds(off, half), 0), (pl.ds(half, half), pl.ds(off, tq), half))
        masks = []
        for _, keys, shift in parts:
            row = lax.broadcasted_iota(jnp.int32, (half, keys.size), 0)
            col = lax.broadcasted_iota(jnp.int32, (half, keys.size), 1)
            masks.append(col <= row + shift)
        queue = [[qk(rows, keys, hh) for rows, keys, _ in parts] for hh in range(ahead)]
        for hh in range(N_HEADS):
            scores = queue.pop(0)
            if hh + ahead < N_HEADS:
                queue.append([qk(rows, keys, hh + ahead) for rows, keys, _ in parts])
            for (rows, keys, _), mask, s in zip(parts, masks, scores):
                s = jnp.where(mask, s, -jnp.inf)
                m = jnp.broadcast_to(jnp.max(s, axis=1, keepdims=True), (half, LANES))
                acc_sc[hh, rows, :] = jnp.dot(probs(s, m), v_ref[0, keys, head(hh)],
                                              preferred_element_type=F32)
                m_sc[hh, rows, :] = m

    def full_block(ki, carry):
        keys = pl.ds(pl.multiple_of(ki * tq, tq), tq)
        rows = pl.ds(0, tq)
        queue = [qk(rows, keys, hh) for hh in range(ahead)]
        for hh in range(N_HEADS):
            s = queue.pop(0)
            if hh + ahead < N_HEADS:
                queue.append(qk(rows, keys, hh + ahead))
            m_prev = m_sc[hh]
            m_next = jnp.maximum(m_prev, jnp.max(s, axis=1, keepdims=True))
            acc_sc[hh] = jnp.exp2(m_prev - m_next) * acc_sc[hh] + jnp.dot(
                probs(s, m_next), v_ref[0, keys, head(hh)], preferred_element_type=F32)
            m_sc[hh] = m_next
        return carry

    diagonal_block(pl.multiple_of(qi * tq, tq))
    lax.fori_loop(0, qi, full_block, 0)

    value_lanes = lax.broadcasted_iota(jnp.int32, (tq, LANES), 1) < V_DIM
    outs = []
    for hh in range(N_HEADS):
        acc = acc_sc[hh]
        row_sum = jnp.where(value_lanes, pltpu.roll(acc, V_DIM, 1), 1.0)
        outs.append(acc / row_sum)
    for pair in range(N_HEADS // 2):
        o_ref[0, :, pair * LANES:(pair + 1) * LANES] = jnp.where(
            value_lanes, outs[2 * pair], pltpu.roll(outs[2 * pair + 1], V_DIM, 1))


def _attn_call(q, k, v):
    b, s, _ = q.shape
    kv_spec = pl.BlockSpec((1, s, D_QK_PAD), lambda bi, qi: (bi, 0, 0))
    return pl.pallas_call(
        _attn_body,
        grid=(b, s // TQ),
        in_specs=[pl.BlockSpec((1, TQ, D_QK_PAD), lambda bi, qi: (bi, qi, 0)), kv_spec, kv_spec],
        out_specs=pl.BlockSpec((1, TQ, D_ATT), lambda bi, qi: (bi, qi, 0)),
        out_shape=jax.ShapeDtypeStruct((b, s, D_ATT), F32),
        scratch_shapes=[pltpu.VMEM((N_HEADS, TQ, LANES), F32)] * 2,
        compiler_params=_params("parallel", "arbitrary"),
        name="attn",
    )(q, k, v)


def _mix_out(rows, a_ref, m_ref, x_ref, ogm_ref, wo_ref, g_ref, b_ref):
    mn = _rms_norm(m_ref[rows, :], ogm_ref[...]).astype(BF16)
    y = jnp.dot(a_ref[rows, :], wo_ref[:D_SG, :], preferred_element_type=F32)
    y = y + jnp.dot(mn, wo_ref[D_SG:, :], preferred_element_type=F32)
    return _layer_norm(DN_ALPHA * x_ref[rows, :] + y, g_ref[...], b_ref[...])


def _sub_rows(ref):
    return [pl.ds(i * SUB, SUB) for i in range(ref.shape[0] // SUB)]


def _store_row_tiles(ref, t0, val):
    nt = val.shape[0]
    for j in range(D_MODEL // LANES):
        ref[pl.ds(ROW_PITCH * t0 + j, nt, stride=ROW_PITCH), :] = val[:, j * LANES:(j + 1) * LANES]
    ref[pl.ds(ROW_PITCH * t0 + SUBLANES, nt, stride=ROW_PITCH), :] = jnp.zeros((nt, LANES), F32)


def _load_row_tiles(ref, t0, nt):
    return jnp.concatenate([ref[pl.ds(ROW_PITCH * t0 + j, nt, stride=ROW_PITCH), :]
                            for j in range(D_MODEL // LANES)], axis=1)


def _oproj_route_body(a_ref, m_ref, x_ref, ogm_ref, wo_ref, g_ref, b_ref, wrh_ref, wrl_ref,
                      x1_ref, x1t_ref, route_ref, cnt_ref, cnt_sc):
    @pl.when(pl.program_id(0) == 0)
    def _():
        cnt_sc[...] = jnp.zeros(cnt_sc.shape, F32)

    lane = lax.broadcasted_iota(jnp.int32, (SUB, LANES), 1)
    lane_f = lane.astype(F32)
    for i, rows in enumerate(_sub_rows(x_ref)):
        x1 = _mix_out(rows, a_ref, m_ref, x_ref, ogm_ref, wo_ref, g_ref, b_ref)
        x1_ref[rows, :] = x1
        _store_row_tiles(x1t_ref, i * SUB, x1)
        x_hi = x1.astype(BF16)
        x_lo = (x1 - x_hi.astype(F32)).astype(BF16)
        logits = (jnp.dot(x_hi, wrh_ref[...], preferred_element_type=F32)
                  + jnp.dot(x_lo, wrh_ref[...], preferred_element_type=F32)
                  + jnp.dot(x_hi, wrl_ref[...], preferred_element_type=F32))
        lg = jnp.where(lane < N_EXPERTS, logits, -jnp.inf)
        m1 = jnp.max(lg, axis=1, keepdims=True)
        i1 = jnp.min(jnp.where(lg == m1, lane_f, float(LANES)), axis=1, keepdims=True)
        lg2 = jnp.where(lane_f == i1, -jnp.inf, lg)
        m2 = jnp.max(lg2, axis=1, keepdims=True)
        i2 = jnp.min(jnp.where(lg2 == m2, lane_f, float(LANES)), axis=1, keepdims=True)
        e = jnp.exp(m2 - m1)
        w1 = 1.0 / (1.0 + e)
        w2 = e / (1.0 + e)
        first_lo = i1 < i2
        e_lo = jnp.where(first_lo, i1, i2)
        e_hi = jnp.where(first_lo, i2, i1)
        w_lo = jnp.where(first_lo, w1, w2)
        w_hi = jnp.where(first_lo, w2, w1)

        onehot = jnp.where((lane_f == e_lo) | (lane_f == e_hi), 1.0, 0.0).astype(BF16)
        cnt_sc[...] += jnp.dot(jnp.ones((SUBLANES, SUB), BF16), onehot,
                               preferred_element_type=F32)
        route = jnp.zeros((SUB, LANES), F32)
        for idx, colv in enumerate((e_lo, e_hi, w_lo, w_hi)):
            route = jnp.where(lane == idx, colv, route)
        route_ref[rows, :] = route
    cnt_ref[...] = cnt_sc[...]


def _oproj_call(a, m, x, ogm, wo, g, b, wr):
    n = x.shape[0]
    in_specs = [_row_spec(TM, D_SG), _row_spec(TM, D_ATT), _row_spec(TM, D_MODEL),
                _const_spec(ogm.shape), _const_spec(wo.shape), _const_spec(g.shape),
                _const_spec(b.shape)]
    wr_hi = wr.astype(BF16)
    wr_lo = (wr - wr_hi.astype(F32)).astype(BF16)
    return pl.pallas_call(
        _oproj_route_body, grid=(n // TM,),
        in_specs=in_specs + [_const_spec(wr_hi.shape), _const_spec(wr_lo.shape)],
        out_specs=[_row_spec(TM, D_MODEL), _row_spec(TM * ROW_PITCH, LANES), _row_spec(TM, LANES),
                   pl.BlockSpec((SUBLANES, LANES), lambda i: (0, 0))],
        out_shape=[jax.ShapeDtypeStruct((n, D_MODEL), F32),
                   jax.ShapeDtypeStruct((n * ROW_PITCH, LANES), F32),
                   jax.ShapeDtypeStruct((n, LANES), F32),
                   jax.ShapeDtypeStruct((SUBLANES, LANES), F32)],
        scratch_shapes=[pltpu.VMEM((SUBLANES, LANES), F32)],
        compiler_params=_params("arbitrary"), name="oproj_route",
    )(a, m, x, ogm, wo, g, b, wr_hi, wr_lo)


def _dense_tail_body(a_ref, m_ref, x_ref, ogm_ref, wo_ref, g1_ref, b1_ref, w1_ref, w3_ref, w2_ref,
                     p_ref, g2_ref, b2_ref, wg_ref, bg_ref, wp_ref, o_ref, x1_sc):
    for rows in _sub_rows(x_ref):
        x1_sc[rows, :] = _mix_out(rows, a_ref, m_ref, x_ref, ogm_ref, wo_ref, g1_ref, b1_ref)
    xb = x1_sc[...].astype(BF16)
    f = None
    for c in range(D_FF // TF_DENSE):
        sl = slice(c * TF_DENSE, (c + 1) * TF_DENSE)
        g = jnp.dot(xb, w1_ref[:, sl], preferred_element_type=F32)
        u = jnp.dot(xb, w3_ref[:, sl], preferred_element_type=F32)
        part = jnp.dot((_silu(g) * u).astype(BF16), w2_ref[sl, :], preferred_element_type=F32)
        f = part if f is None else f + part
    for i, rows in enumerate(_sub_rows(x_ref)):
        _ple_tail(rows, x1_sc[rows, :], f[i * SUB:(i + 1) * SUB, :], p_ref, g2_ref, b2_ref,
                  wg_ref, bg_ref, wp_ref, o_ref)


def _ple_spec(layer, n):
    return pl.BlockSpec((TM, D_PLE), lambda i: (i + layer * (n // TM), 0))


def _dense_tail_call(layer, a, m, x, ogm, wo, g1, b1, w1, w3, w2, p, g2, b2, wg, bg, wp):
    n = x.shape[0]
    consts = (ogm, wo, g1, b1, w1, w3, w2)
    tail_consts = (g2, b2, wg, bg, wp)
    return pl.pallas_call(
        _dense_tail_body, grid=(n // TM,),
        in_specs=[_row_spec(TM, D_SG), _row_spec(TM, D_ATT), _row_spec(TM, D_MODEL)]
        + [_const_spec(c.shape) for c in consts] + [_ple_spec(layer, n)]
        + [_const_spec(c.shape) for c in tail_consts],
        out_specs=_row_spec(TM, D_MODEL),
        out_shape=jax.ShapeDtypeStruct((n, D_MODEL), F32),
        scratch_shapes=[pltpu.VMEM((TM, D_MODEL), F32)],
        compiler_params=_params("parallel"), name="dense_tail",
    )(a, m, x, *consts, p, *tail_consts)


def _moe_body(layer, te_ref, tv_ref, tf_ref, src0_ref, src1_ref, dst_ref, x_hbm, w1_hbm, w3_hbm,
              w2_hbm, y_hbm, xbuf, xb_sc, ybuf, wb1, wb3, wb2, st1, st3, st2, gsem, ssem, wsem):
    t = pl.program_id(0)
    tm = xb_sc.shape[0]
    n_chunks = wb1.shape[0]
    rows_per_chunk = -(-tm // (n_chunks - 2))
    expert = te_ref[t]
    n_tile_rows = ROW_PITCH * tm

    def dma_thread(r):
        return r % 2 if isinstance(r, int) else 0

    def start_gather(src_ref, lo, hi):
        for r in range(lo, hi) if isinstance(lo, int) else (lo,):
            pltpu.make_async_copy(x_hbm.at[pl.ds(ROW_PITCH * src_ref[0, 0, r], ROW_PITCH), :],
                                  xbuf.at[pl.ds(ROW_PITCH * r, ROW_PITCH), :],
                                  gsem).start(priority=dma_thread(r))

    def start_scatter(lo, hi):
        for r in range(lo, hi) if isinstance(lo, int) else (lo,):
            pltpu.make_async_copy(ybuf.at[pl.ds(ROW_PITCH * r, ROW_PITCH), :],
                                  y_hbm.at[pl.ds(ROW_PITCH * dst_ref[0, 0, r], ROW_PITCH), :],
                                  ssem).start(priority=dma_thread(r))

    def start_rolled(start_rows):
        def body(r, carry):
            start_rows(r, r + 1)
            return carry
        lax.fori_loop(0, tm, body, 0)

    def wait_gather():
        pltpu.make_async_copy(x_hbm.at[pl.ds(0, n_tile_rows), :], xbuf.at[pl.ds(0, n_tile_rows), :],
                              gsem).wait()

    def wait_scatter():
        pltpu.make_async_copy(ybuf.at[pl.ds(0, n_tile_rows), :], y_hbm.at[pl.ds(0, n_tile_rows), :],
                              ssem).wait()

    def weight_copies(c, slot):
        sl = pl.ds(c * TF_MOE, TF_MOE)
        sem = wsem.at[slot]
        return (pltpu.make_async_copy(w1_hbm.at[layer, expert, :, sl], st1.at[slot], sem),
                pltpu.make_async_copy(w3_hbm.at[layer, expert, :, sl], st3.at[slot], sem),
                pltpu.make_async_copy(w2_hbm.at[layer, expert, sl, :], st2.at[slot], sem))

    def row_copies_after(c, part):
        pin = part[0:SUBLANES, 0:LANES]
        xbuf[n_tile_rows:n_tile_rows + SUBLANES, :] = pin
        ybuf[n_tile_rows:n_tile_rows + SUBLANES, :] = pin
        lo, hi = min(c * rows_per_chunk, tm), min((c + 1) * rows_per_chunk, tm)
        start_gather(src1_ref, lo, hi)
        start_scatter(lo, hi)

    def swiglu_tile(load_weights):
        xb = xb_sc[...]
        if load_weights:
            for c in range(2):
                for cp in weight_copies(c, c):
                    cp.start()
        acc = None
        for c in range(n_chunks):
            if load_weights:
                slot = c % 2
                for cp in weight_copies(c, slot):
                    cp.wait()
                w1c = st1[slot].astype(BF16)
                w3c = st3[slot].astype(BF16)
                w2c = st2[slot].astype(BF16)
                wb1[c] = w1c
                wb3[c] = w3c
                wb2[c] = w2c
                if c + 2 < n_chunks:
                    for cp in weight_copies(c + 2, slot):
                        cp.start()
            else:
                w1c, w3c, w2c = wb1[c], wb3[c], wb2[c]
            g = jnp.dot(xb, w1c, preferred_element_type=F32)
            u = jnp.dot(xb, w3c, preferred_element_type=F32)
            part = jnp.dot((_silu(g) * u).astype(BF16), w2c, preferred_element_type=F32)
            acc = part if acc is None else acc + part
            row_copies_after(c, part)
        wait_scatter()
        _store_row_tiles(ybuf, 0, acc)

    @pl.when(t == 0)
    def _():
        ybuf[...] = jnp.zeros(ybuf.shape, F32)
        start_rolled(functools.partial(start_gather, src0_ref))

    wait_gather()
    xb_sc[...] = _load_row_tiles(xbuf, 0, tm).astype(BF16)
    valid = tv_ref[t] == 1
    first = tf_ref[t] == 1

    @pl.when(valid & first)
    def _():
        swiglu_tile(True)

    @pl.when(valid & jnp.logical_not(first))
    def _():
        swiglu_tile(False)

    @pl.when(jnp.logical_not(valid))
    def _():
        start_rolled(functools.partial(start_gather, src1_ref))
        start_rolled(start_scatter)
        wait_scatter()

    @pl.when(t == pl.num_programs(0) - 1)
    def _():
        wait_gather()


def _moe_call(layer, tile_expert, tile_valid, tile_first, src_rows, dst_rows, x1, w1, w3, w2):
    n = x1.shape[0] // ROW_PITCH
    n_steps = dst_rows.shape[0]
    n_chunks = D_FF_EXPERT // TF_MOE

    def row_block(offset):
        return pl.BlockSpec((1, 1, TM_MOE), lambda t, te, tv, tf: (t + offset, 0, 0),
                            memory_space=pltpu.SMEM)

    hbm = pl.BlockSpec(memory_space=pl.ANY)
    return pl.pallas_call(
        functools.partial(_moe_body, layer),
        grid_spec=pltpu.PrefetchScalarGridSpec(
            num_scalar_prefetch=3,
            grid=(n_steps,),
            in_specs=[row_block(0), row_block(1), row_block(0), hbm, hbm, hbm, hbm],
            out_specs=hbm,
            scratch_shapes=[pltpu.VMEM((TM_MOE * ROW_PITCH + SUBLANES, LANES), F32),
                            pltpu.VMEM((TM_MOE, D_MODEL), BF16),
                            pltpu.VMEM((TM_MOE * ROW_PITCH + SUBLANES, LANES), F32),
                            pltpu.VMEM((n_chunks, D_MODEL, TF_MOE), BF16),
                            pltpu.VMEM((n_chunks, D_MODEL, TF_MOE), BF16),
                            pltpu.VMEM((n_chunks, TF_MOE, D_MODEL), BF16),
                            pltpu.VMEM((2, D_MODEL, TF_MOE), F32),
                            pltpu.VMEM((2, D_MODEL, TF_MOE), F32),
                            pltpu.VMEM((2, TF_MOE, D_MODEL), F32),
                            pltpu.SemaphoreType.DMA, pltpu.SemaphoreType.DMA,
                            pltpu.SemaphoreType.DMA((2,))],
        ),
        out_shape=jax.ShapeDtypeStruct(((2 * n + TM_MOE) * ROW_PITCH, LANES), F32),
        compiler_params=_params("arbitrary"), name="moe_experts",
    )(tile_expert, tile_valid, tile_first, src_rows, src_rows, dst_rows, x1, w1, w3, w2)


def _route_plan(route, counts):
    n = route.shape[0]
    n_tiles = 2 * n // TM_MOE + N_EXPERTS - 1
    cnt = counts[0, :N_EXPERTS].astype(jnp.int32)
    tiles = (cnt + TM_MOE - 1) // TM_MOE
    tile_end = jnp.cumsum(tiles)
    row0 = (tile_end - tiles) * TM_MOE
    first_asg = jnp.cumsum(cnt) - cnt
    t_idx = jnp.arange(n_tiles + 1)
    tile_valid = (t_idx < tile_end[-1]).astype(jnp.int32)
    tile_expert = jnp.minimum(jnp.sum(t_idx[:, None] >= tile_end[None, :], axis=1),
                              N_EXPERTS - 1).astype(jnp.int32)
    tile_first = jnp.concatenate([jnp.ones((1,), jnp.int32),
                                  (tile_expert[1:] != tile_expert[:-1]).astype(jnp.int32)])
    e_pair = route[:, 0:2].astype(jnp.int32)
    tok = jnp.arange(n, dtype=jnp.int32)
    keys = (e_pair * n + tok[:, None]) * 2 + jnp.arange(2, dtype=jnp.int32)[None, :]
    keys = jnp.sort(keys.reshape(-1))
    asg_row = (keys % 2) * n + (keys // 2) % n
    te = tile_expert[:n_tiles]
    col = jnp.arange(TM_MOE, dtype=jnp.int32)[None, :]
    r = (jnp.arange(n_tiles, dtype=jnp.int32) * TM_MOE - row0[te])[:, None] + col
    occupied = r < cnt[te][:, None]
    asg = jnp.clip(first_asg[te][:, None] + r, 0, 2 * n - 1)
    spare = jnp.broadcast_to(2 * n + col, (n_tiles, TM_MOE))
    dst_rows = jnp.where(occupied, asg_row[asg], spare)
    src_rows = jnp.where(occupied, dst_rows % n, dst_rows - 2 * n)
    dst_rows = jnp.concatenate([spare[:1], dst_rows])
    src_rows = jnp.concatenate([src_rows, jnp.zeros((2, TM_MOE), jnp.int32)])
    return (tile_expert, tile_valid, tile_first, src_rows.reshape(n_tiles + 2, 1, TM_MOE),
            dst_rows.reshape(n_tiles + 1, 1, TM_MOE))


def _ple_tail(rows, x1, f, p_ref, g_ref, b_ref, wg_ref, bg_ref, wp_ref, o_ref):
    x2 = _layer_norm(DN_ALPHA * x1 + f, g_ref[...], b_ref[...])
    gate = jax.nn.sigmoid(
        jnp.dot(x2.astype(BF16), wg_ref[...], preferred_element_type=F32) + bg_ref[...])
    emb = jnp.dot(p_ref[rows, :].astype(BF16), wp_ref[...], preferred_element_type=F32)
    o_ref[rows, :] = x2 + gate * emb


def _ple_moe_body(x1_ref, ylo_ref, yhi_ref, route_ref, *rest):
    for i, rows in enumerate(_sub_rows(x1_ref)):
        route = route_ref[rows, :]
        f = (route[:, 2:3] * _load_row_tiles(ylo_ref, i * SUB, SUB)
             + route[:, 3:4] * _load_row_tiles(yhi_ref, i * SUB, SUB))
        _ple_tail(rows, x1_ref[rows, :], f, *rest)


def _ple_call(layer, x1, y, p, g, b, wg, bg, wp, route):
    n = x1.shape[0]
    hi_spec = pl.BlockSpec((TM * ROW_PITCH, LANES), lambda i: (i + n // TM, 0))
    return pl.pallas_call(
        _ple_moe_body, grid=(n // TM,),
        in_specs=[_row_spec(TM, D_MODEL), _row_spec(TM * ROW_PITCH, LANES), hi_spec,
                  _row_spec(TM, LANES),
                  _ple_spec(layer, n), _const_spec(g.shape), _const_spec(b.shape),
                  _const_spec(wg.shape), _const_spec(bg.shape), _const_spec(wp.shape)],
        out_specs=_row_spec(TM, D_MODEL),
        out_shape=jax.ShapeDtypeStruct((n, D_MODEL), F32),
        compiler_params=_params("parallel"), name="ln_ple",
    )(x1, y, y, route, p, g, b, wg, bg, wp)


def _rope_tables(positions):
    inv_freq = 1.0 / (ROPE_THETA ** (jnp.arange(0, QK_ROPE, 2, dtype=F32) / QK_ROPE))
    lane = jnp.arange(LANES)
    rotary = (lane >= ROPE_LANE0) & (lane < ROPE_LANE0 + QK_ROPE)
    freq = jnp.where(rotary, inv_freq[(lane - ROPE_LANE0) % ROPE_HALF], 0.0)
    sign = jnp.where(lane < ROPE_LANE0 + ROPE_HALF, -1.0, 1.0).astype(F32)
    ang = positions.astype(F32).reshape(-1, 1) * freq[None, :]
    return jnp.cos(ang), jnp.sin(ang) * sign[None, :]


def _row(v):
    return v.reshape(1, -1).astype(F32)


def kernel(x, p, positions, w_in, sg_v_g, sg_v_b, sg_w_s, sg_b_s, q_norm_g, kv_norm_g, w_uq, w_ukv, out_g, w_o, ln1_g, ln1_b, ffn_w1, ffn_w3, ffn_w2, moe_w_r, moe_w1, moe_w3, moe_w2, ln2_g, ln2_b, ple_w_g, ple_b_g, ple_w_p):
    bsz, seq, _ = x.shape
    n = bsz * seq
    xf = x.reshape(n, D_MODEL)
    rc, rs = _rope_tables(positions)

    for i in range(DEPTH):
        win = jnp.concatenate(
            [w_in[i, :, :C_KR0], jnp.zeros((D_MODEL, ROPE_LANE0), F32), w_in[i, :, C_KR0:],
             jnp.zeros((D_MODEL, LANES - ROPE_LANE0 - QK_ROPE), F32)], axis=1).astype(BF16)
        wuq = jnp.pad(w_uq[i].reshape(Q_LORA, N_HEADS, QK_NOPE + QK_ROPE),
                      ((0, 0), (0, 0), (0, HEAD_PAD - QK_NOPE - QK_ROPE)))
        wuq = wuq.reshape(Q_LORA, D_QK_PAD).astype(BF16)
        wukv = w_ukv[i].reshape(KV_LORA, N_HEADS, QK_NOPE + V_DIM)
        wuk = jnp.pad(wukv[:, :, :QK_NOPE], ((0, 0), (0, 0), (0, HEAD_PAD - QK_NOPE)))
        wuk = wuk.reshape(KV_LORA, D_QK_PAD).astype(BF16)
        wuv = jnp.pad(wukv[:, :, QK_NOPE:], ((0, 0), (0, 0), (0, HEAD_PAD - V_DIM)))
        wuv = wuv.reshape(KV_LORA, D_QK_PAD).astype(BF16)
        bexp = jnp.repeat(sg_b_s[i].T, SG_DIM, axis=1)
        a, q, k, v = _proj_call(xf, win, _row(sg_v_g[i]), _row(sg_v_b[i]), sg_w_s[i], bexp,
                                _row(q_norm_g[i]), _row(kv_norm_g[i]), wuq, wuk, wuv, rc, rs,
                                _row(out_g[i, :D_SG]))
        m = _attn_call(q.reshape(bsz, seq, D_QK_PAD), k.reshape(bsz, seq, D_QK_PAD),
                       v.reshape(bsz, seq, D_QK_PAD)).reshape(n, D_ATT)

        j = i // 2
        mix_args = (a, m, xf, _row(out_g[i, D_SG:]), w_o[i].astype(BF16), _row(ln1_g[i]),
                    _row(ln1_b[i]))
        ple_args = (p.reshape(DEPTH * n, D_PLE), _row(ln2_g[i]), _row(ln2_b[i]),
                    ple_w_g[i].astype(BF16), _row(ple_b_g[i]), ple_w_p[i].astype(BF16))
        if i % 2 == 0:
            xf = _dense_tail_call(i, *mix_args, ffn_w1[j].astype(BF16), ffn_w3[j].astype(BF16),
                                  ffn_w2[j].astype(BF16), *ple_args)
        else:
            wr = jnp.pad(moe_w_r[j], ((0, 0), (0, LANES - N_EXPERTS)))
            x1, x1t, route, counts = _oproj_call(*mix_args, wr)
            f = _moe_call(j, *_route_plan(route, counts), x1t, moe_w1, moe_w3, moe_w2)
            xf = _ple_call(i, x1, f, *ple_args, route)

    return xf.reshape(bsz, seq, D_MODEL)
```
